```python
import jax, jax.numpy as jnp
from jax import lax
import numpy as np

D_MODEL = 2048
BATCH = 4
SEQ = 2048
DEPTH = 1
DEC_BATCH = 32
DEC_SEQ = 8
PAST_LEN = 8192
PAGE_SIZE = 128

HEAD_DIM = 64
ATT_SLOTS = 16
DIL_PATTERNS = ((128, 1), (512, 4), (2048, 16))
N_GROUPS = len(DIL_PATTERNS)
ATT_WIDTH = ATT_SLOTS * HEAD_DIM
QKV_WIDTH = N_GROUPS * ATT_SLOTS * HEAD_DIM
CONV_WIDTH = D_MODEL - ATT_WIDTH
CONV_K = 3
IN_WIDTH = 3 * QKV_WIDTH + 3 * CONV_WIDTH
N_KEYS = 128
N_EXPERTS = N_KEYS * N_KEYS
PEER_HEADS = 8
PEER_QDIM = 256
PEER_HALF = PEER_QDIM // 2
PEER_TOPK = 16
PEER_BLOCK = 128
NORM_EPS = 1e-6
NEG_INF = -1e30

kernel_name = 'hymba_dilated_attn_shortconv_peer_step'


def rmsnorm(x, g):
    xf = x.astype(jnp.float32)
    y = xf * lax.rsqrt(jnp.mean(xf * xf, axis=-1, keepdims=True) + NORM_EPS)
    return (y * g.astype(jnp.float32)).astype(x.dtype)


def alibi_slopes():
    n = N_GROUPS * ATT_SLOTS
    i = jnp.arange(1, n + 1, dtype=jnp.float32)
    return jnp.exp2(-8.0 * i / n).reshape(N_GROUPS, ATT_SLOTS)


def split_projection(x, norm_g, w_in):
    xn = rmsnorm(x, norm_g)
    p = xn @ w_in
    cuts = [QKV_WIDTH, 2 * QKV_WIDTH, 3 * QKV_WIDTH,
            3 * QKV_WIDTH + CONV_WIDTH, 3 * QKV_WIDTH + 2 * CONV_WIDTH]
    q, k, v, gate_b, gate_c, h = jnp.split(p, cuts, axis=-1)
    heads = x.shape[:2] + (N_GROUPS, ATT_SLOTS, HEAD_DIM)
    return q.reshape(heads), k.reshape(heads), v.reshape(heads), gate_b, gate_c * h


def causal_conv(up, w, n_out):
    y = up[:, 0:n_out] * w[0]
    for j in range(1, CONV_K):
        y = y + up[:, j:j + n_out] * w[j]
    return y


def to_sub(a, dil, blk, seq_pad):
    b, s = a.shape[:2]
    rest = a.shape[2:]
    a = jnp.pad(a, ((0, 0), (0, seq_pad - s)) + ((0, 0),) * len(rest))
    a = a.reshape((b, seq_pad // dil, dil) + rest)
    a = jnp.moveaxis(a, 2, 1)
    return a.reshape((b, dil, seq_pad // (dil * blk), blk) + rest)


def from_sub(a, dil, seq):
    b = a.shape[0]
    rest = a.shape[4:]
    a = a.reshape((b, dil, -1) + rest)
    a = jnp.moveaxis(a, 1, 2)
    return a.reshape((b, -1) + rest)[:, :seq]


def dilated_prompt(q, k, v, window, dil, slopes):
    b, s, h, dh = q.shape
    n_back = window // dil
    blk = n_back
    span = dil * blk
    seq_pad = -(-s // span) * span
    qb, kb, vb = (to_sub(a, dil, blk, seq_pad) for a in (q, k, v))
    nb = seq_pad // span

    def with_prev(a):
        prev = jnp.pad(a, ((0, 0), (0, 0), (1, 0), (0, 0), (0, 0), (0, 0)))[:, :, :-1]
        return jnp.concatenate([prev, a], axis=3)

    kk, vv = with_prev(kb), with_prev(vb)
    blocks = jnp.arange(nb)[:, None, None]
    qi = blocks * blk + jnp.arange(blk)[None, :, None]
    kj = (blocks - 1) * blk + jnp.arange(2 * blk)[None, None, :]
    steps = qi - kj
    valid = (steps >= 0) & (steps <= n_back) & (kj >= 0)
    dist = (steps * dil).astype(jnp.float32)
    bias = jnp.where(valid[:, None], -slopes[None, :, None, None] * dist[:, None], NEG_INF)
    scale = HEAD_DIM ** -0.5
    sc = jnp.einsum('brnqhd,brnkhd->brnhqk', qb, kk).astype(jnp.float32) * scale + bias
    lse = jax.nn.logsumexp(sc, axis=-1)
    p = jnp.exp(sc - lse[..., None])
    o = jnp.einsum('brnhqk,brnkhd->brnqhd', p, vv.astype(jnp.float32))
    o = from_sub(o, dil, s).astype(q.dtype)
    lse = from_sub(jnp.moveaxis(lse, 3, 4), dil, s)
    return o, lse


def dilated_sample(q, k_all, v_all, window, dil, slopes, n_prev):
    t = q.shape[1]
    n_back = window // dil
    steps = jnp.arange(n_back + 1)
    idx = n_prev + jnp.arange(t)[:, None] - steps[None, :] * dil
    valid = idx >= 0
    idx_c = jnp.maximum(idx, 0)
    kg = k_all[:, idx_c]
    vg = v_all[:, idx_c]
    dist = (steps * dil).astype(jnp.float32)
    bias = jnp.where(valid[None], -slopes[:, None, None] * dist[None, None, :], NEG_INF)
    scale = HEAD_DIM ** -0.5
    sc = jnp.einsum('bthd,btkhd->bhtk', q, kg).astype(jnp.float32) * scale + bias
    lse = jax.nn.logsumexp(sc, axis=-1)
    p = jnp.exp(sc - lse[..., None])
    o = jnp.einsum('bhtk,btkhd->bthd', p, vg.astype(jnp.float32)).astype(q.dtype)
    return o, jnp.transpose(lse, (0, 2, 1))


def combine_groups(outs, lses):
    o = jnp.stack(outs, 0).astype(jnp.float32)
    w = jax.nn.softmax(jnp.stack(lses, 0), axis=0)
    return jnp.einsum('gnsh,gnshd->nshd', w, o).astype(outs[0].dtype)


def mixer_output(att, conv_y, g_att, g_conv, w_out):
    n, s = att.shape[:2]
    cat = jnp.concatenate([rmsnorm(att.reshape(n, s, ATT_WIDTH), g_att),
                           rmsnorm(conv_y, g_conv)], axis=-1)
    return cat @ w_out


def peer_ffn(hn, w_pq, sub_keys_a, sub_keys_b, expert_u, expert_v):
    n, s, d = hn.shape
    x = hn.reshape(n * s, d)
    tokens = n * s
    pad = (-tokens) % PEER_BLOCK
    xp = jnp.pad(x, ((0, pad), (0, 0))).reshape(-1, PEER_BLOCK, d)
    ka = sub_keys_a.astype(jnp.float32)
    kb = sub_keys_b.astype(jnp.float32)

    def block(xb):
        q = (xb @ w_pq).astype(jnp.float32).reshape(PEER_BLOCK, PEER_HEADS, PEER_QDIM)
        sa = jnp.einsum('thc,kc->thk', q[..., :PEER_HALF], ka)
        sb = jnp.einsum('thc,kc->thk', q[..., PEER_HALF:], kb)
        va, ia = lax.top_k(sa, PEER_TOPK)
        vb, ib = lax.top_k(sb, PEER_TOPK)
        cand = (va[..., :, None] + vb[..., None, :]).reshape(PEER_BLOCK, PEER_HEADS, -1)
        cid = (ia[..., :, None] * N_KEYS + ib[..., None, :]).reshape(PEER_BLOCK, PEER_HEADS, -1)
        top_s, top_i = lax.top_k(cand, PEER_TOPK)
        eid = jnp.take_along_axis(cid, top_i, axis=-1)
        gate = jax.nn.softmax(top_s, axis=-1)
        u = expert_u[eid]
        act = jax.nn.gelu(jnp.einsum('thkd,td->thk', u, xb).astype(jnp.float32), approximate=False)
        v = expert_v[eid]
        return jnp.einsum('thk,thkd->td', (gate * act).astype(xb.dtype), v)

    out = lax.map(block, xp).reshape(-1, d)[:tokens]
    return out.reshape(n, s, d)


def setup_inputs(seed: int = 0) -> dict:
    key = jax.random.key(seed)
    ks = jax.random.split(key, 24)
    f32 = jnp.float32
    nrm = lambda k, shape, sc: jax.random.normal(k, shape, f32) * sc
    gain = lambda k, shape: 1.0 + 0.02 * jax.random.normal(k, shape, f32)
    rows = [min(w, PAST_LEN) for (w, _) in DIL_PATTERNS]
    kv_shape = lambda r: (DEPTH, DEC_BATCH, r, ATT_SLOTS, HEAD_DIM)
    return {
        'x_prompt': nrm(ks[0], (BATCH, SEQ, D_MODEL), 1.0),
        'x_sample': nrm(ks[1], (DEC_BATCH, DEC_SEQ, D_MODEL), 1.0),
        'cache_k_g0': nrm(ks[2], kv_shape(rows[0]), 1.0),
        'cache_v_g0': nrm(ks[3], kv_shape(rows[0]), 1.0),
        'cache_k_g1': nrm(ks[4], kv_shape(rows[1]), 1.0),
        'cache_v_g1': nrm(ks[5], kv_shape(rows[1]), 1.0),
        'cache_k_g2': nrm(ks[6], kv_shape(rows[2]), 1.0),
        'cache_v_g2': nrm(ks[7], kv_shape(rows[2]), 1.0),
        'state_conv': nrm(ks[8], (DEPTH, DEC_BATCH, CONV_K - 1, CONV_WIDTH), 1.0),
        'norm_mix': gain(ks[9], (DEPTH, D_MODEL)),
        'w_in': nrm(ks[10], (DEPTH, D_MODEL, IN_WIDTH), D_MODEL ** -0.5),
        'conv_w': nrm(ks[11], (DEPTH, CONV_K, CONV_WIDTH), CONV_K ** -0.5),
        'norm_att_out': gain(ks[12], (DEPTH, ATT_WIDTH)),
        'norm_conv_out': gain(ks[13], (DEPTH, CONV_WIDTH)),
        'w_out': nrm(ks[14], (DEPTH, D_MODEL, D_MODEL), D_MODEL ** -0.5),
        'norm_ffn': gain(ks[15], (DEPTH, D_MODEL)),
        'w_pq': nrm(ks[16], (DEPTH, D_MODEL, PEER_HEADS * PEER_QDIM), D_MODEL ** -0.5),
        'sub_keys_a': nrm(ks[17], (DEPTH, N_KEYS, PEER_HALF), PEER_HALF ** -0.5),
        'sub_keys_b': nrm(ks[18], (DEPTH, N_KEYS, PEER_HALF), PEER_HALF ** -0.5),
        'expert_u': nrm(ks[19], (DEPTH, N_EXPERTS, D_MODEL), D_MODEL ** -0.5),
        'expert_v': nrm(ks[20], (DEPTH, N_EXPERTS, D_MODEL), PEER_HEADS ** -0.5),
        'norm_final': gain(ks[21], (D_MODEL,)),
    }


def reference(x_prompt, x_sample, cache_k_g0, cache_v_g0, cache_k_g1, cache_v_g1,
              cache_k_g2, cache_v_g2, state_conv, norm_mix, w_in, conv_w, norm_att_out,
              norm_conv_out, w_out, norm_ffn, w_pq, sub_keys_a, sub_keys_b, expert_u,
              expert_v, norm_final):
    slopes = alibi_slopes()
    caches_k = (cache_k_g0, cache_k_g1, cache_k_g2)
    caches_v = (cache_v_g0, cache_v_g1, cache_v_g2)
    s_len = x_prompt.shape[1]
    t_len = x_sample.shape[1]
    nk_p = [[] for _ in range(N_GROUPS)]
    nv_p = [[] for _ in range(N_GROUPS)]
    nk_s = [[] for _ in range(N_GROUPS)]
    nv_s = [[] for _ in range(N_GROUPS)]
    nc_p, nc_s = [], []
    hp, hs = x_prompt, x_sample
    for l in range(DEPTH):
        q, k, v, gate_b, u = split_projection(hp, norm_mix[l], w_in[l])
        outs, lses = [], []
        for g, (win, dil) in enumerate(DIL_PATTERNS):
            o, lse = dilated_prompt(q[:, :, g], k[:, :, g], v[:, :, g], win, dil, slopes[g])
            outs.append(o)
            lses.append(lse)
            rows = min(win, s_len)
            nk_p[g].append(k[:, s_len - rows:, g])
            nv_p[g].append(v[:, s_len - rows:, g])
        up = jnp.pad(u, ((0, 0), (CONV_K - 1, 0), (0, 0)))
        conv_y = gate_b * causal_conv(up, conv_w[l], s_len)
        nc_p.append(up[:, up.shape[1] - (CONV_K - 1):])
        hp = hp + mixer_output(combine_groups(outs, lses), conv_y,
                               norm_att_out[l], norm_conv_out[l], w_out[l])
        hp = hp + peer_ffn(rmsnorm(hp, norm_ffn[l]), w_pq[l], sub_keys_a[l], sub_keys_b[l],
                           expert_u[l], expert_v[l])

        q, k, v, gate_b, u = split_projection(hs, norm_mix[l], w_in[l])
        outs, lses = [], []
        for g, (win, dil) in enumerate(DIL_PATTERNS):
            k_all = jnp.concatenate([caches_k[g][l], k[:, :, g]], axis=1)
            v_all = jnp.concatenate([caches_v[g][l], v[:, :, g]], axis=1)
            o, lse = dilated_sample(q[:, :, g], k_all, v_all, win, dil, slopes[g],
                                    caches_k[g].shape[2])
            outs.append(o)
            lses.append(lse)
            rows = min(win, k_all.shape[1])
            nk_s[g].append(k_all[:, k_all.shape[1] - rows:])
            nv_s[g].append(v_all[:, v_all.shape[1] - rows:])
        up = jnp.concatenate([state_conv[l], u], axis=1)
        conv_y = gate_b * causal_conv(up, conv_w[l], t_len)
        nc_s.append(up[:, up.shape[1] - (CONV_K - 1):])
        hs = hs + mixer_output(combine_groups(outs, lses), conv_y,
                               norm_att_out[l], norm_conv_out[l], w_out[l])
        hs = hs + peer_ffn(rmsnorm(hs, norm_ffn[l]), w_pq[l], sub_keys_a[l], sub_keys_b[l],
                           expert_u[l], expert_v[l])
    y_prompt = rmsnorm(hp, norm_final)
    y_sample = rmsnorm(hs, norm_final)
    return (y_prompt, y_sample,
            jnp.stack(nk_p[0], 0), jnp.stack(nv_p[0], 0),
            jnp.stack(nk_p[1], 0), jnp.stack(nv_p[1], 0),
            jnp.stack(nk_p[2], 0), jnp.stack(nv_p[2], 0),
            jnp.stack(nc_p, 0),
            jnp.stack(nk_s[0], 0), jnp.stack(nv_s[0], 0),
            jnp.stack(nk_s[1], 0), jnp.stack(nv_s[1], 0),
            jnp.stack(nk_s[2], 0), jnp.stack(nv_s[2], 0),
            jnp.stack(nc_s, 0))
```

```python
import functools
import math

import jax
import jax.numpy as jnp
from jax import lax
from jax.experimental import pallas as pl
from jax.experimental.pallas import tpu as pltpu

HEAD_DIM = 64
ATT_SLOTS = 16
DIL_PATTERNS = ((128, 1), (512, 4), (2048, 16))
N_GROUPS = len(DIL_PATTERNS)
ATT_WIDTH = ATT_SLOTS * HEAD_DIM
QKV_WIDTH = N_GROUPS * ATT_WIDTH
CONV_K = 3
N_KEYS = 128
PEER_HEADS = 8
PEER_QDIM = 256
PEER_HALF = PEER_QDIM // 2
PEER_TOPK = 16
NORM_EPS = 1e-6
NEG_INF = -1e30

LANES = 128
HEADS_PER_SLAB = LANES // HEAD_DIM
N_SLABS = ATT_WIDTH // LANES
N_BACK = 128
VMEM_LIMIT = 56 * 1024 * 1024

_NT = (((1,), (1,)), ((), ()))
_TN = (((0,), (0,)), ((), ()))


def _params(vmem=None):
    return pltpu.CompilerParams(vmem_limit_bytes=vmem or VMEM_LIMIT)


def _rms(x, g):
    return x * lax.rsqrt(jnp.mean(x * x, axis=-1, keepdims=True) + NORM_EPS) * g


def _rmsnorm_kernel(x_ref, g_ref, o_ref):
    o_ref[...] = _rms(x_ref[...], g_ref[...]).astype(o_ref.dtype)


def _rmsnorm(x, g, out_dtype, tm):
    t, d = x.shape
    return pl.pallas_call(
        _rmsnorm_kernel,
        grid=(t // tm,),
        in_specs=[pl.BlockSpec((tm, d), lambda i: (i, 0)),
                  pl.BlockSpec((1, d), lambda i: (0, 0))],
        out_specs=pl.BlockSpec((tm, d), lambda i: (i, 0)),
        out_shape=jax.ShapeDtypeStruct((t, d), out_dtype),
        compiler_params=_params(),
        name="rmsnorm",
    )(x, g.reshape(1, d))


def _mm_kernel(a_ref, b_ref, o_ref):
    o_ref[...] = jnp.dot(a_ref[...], b_ref[...],
                         preferred_element_type=jnp.float32).astype(o_ref.dtype)


def _matmul(a, b, col_start, n_cols, out_dtype, tm, tn):
    t, k = a.shape
    off = col_start // tn
    return pl.pallas_call(
        _mm_kernel,
        grid=(t // tm, n_cols // tn),
        in_specs=[pl.BlockSpec((tm, k), lambda i, j: (i, 0)),
                  pl.BlockSpec((k, tn), lambda i, j: (0, j + off))],
        out_specs=pl.BlockSpec((tm, tn), lambda i, j: (i, j)),
        out_shape=jax.ShapeDtypeStruct((t, n_cols), out_dtype),
        compiler_params=_params(),
        name="matmul",
    )(a, b)


def _conv_kernel(gb_ref, gc_ref, h_ref, st_ref, w_ref, y_ref, ns_ref):
    u = gc_ref[...] * h_ref[...]
    s = u.shape[0]
    row = lax.broadcasted_iota(jnp.int32, u.shape, 0)
    st = st_ref[...]
    w = w_ref[...]
    u1 = jnp.where(row == 0, st[1:2, :], pltpu.roll(u, 1, 0))
    u2 = jnp.where(row == 0, st[0:1, :],
                   jnp.where(row == 1, st[1:2, :], pltpu.roll(u, 2, 0)))
    y = u2 * w[0:1, :]
    y = y + u1 * w[1:2, :]
    y = y + u * w[2:3, :]
    y_ref[...] = gb_ref[...] * y
    ns_ref[...] = u[s - 2:s, :]


def _gated_conv(pc, state, conv_w, n_seq, seq, cb):
    c = conv_w.shape[1]
    ncb = c // cb
    return pl.pallas_call(
        _conv_kernel,
        grid=(n_seq, ncb),
        in_specs=[pl.BlockSpec((seq, cb), lambda b, j: (b, j)),
                  pl.BlockSpec((seq, cb), lambda b, j: (b, ncb + j)),
                  pl.BlockSpec((seq, cb), lambda b, j: (b, 2 * ncb + j)),
                  pl.BlockSpec((None, CONV_K - 1, cb), lambda b, j: (b, 0, j)),
                  pl.BlockSpec((CONV_K, cb), lambda b, j: (0, j))],
        out_specs=[pl.BlockSpec((seq, cb), lambda b, j: (b, j)),
                   pl.BlockSpec((None, CONV_K - 1, cb), lambda b, j: (b, 0, j))],
        out_shape=[jax.ShapeDtypeStruct((n_seq * seq, c), jnp.float32),
                   jax.ShapeDtypeStruct((n_seq, CONV_K - 1, c), jnp.float32)],
        compiler_params=_params(),
        name="gated_conv",
    )(pc, pc, pc, state, conv_w)


def _slope_like(shape, group, head):
    i = (group * ATT_SLOTS + 1 + head).astype(jnp.float32)
    n = float(N_GROUPS * ATT_SLOTS)
    return jnp.exp2(jnp.broadcast_to(-8.0 * i / n, shape))


def _prompt_attn_kernel(q0, k0, v0, q1, k1, v1, q2, k2, v2, o_ref,
                        m_acc, l_acc, o_acc, *, seq):
    slab = pl.program_id(1)
    lane = lax.broadcasted_iota(jnp.int32, (1, LANES), 1)
    first = lane < HEAD_DIM
    scale = HEAD_DIM ** -0.5

    m_acc[...] = jnp.full(m_acc.shape, NEG_INF, jnp.float32)
    l_acc[...] = jnp.zeros(l_acc.shape, jnp.float32)
    o_acc[...] = jnp.zeros(o_acc.shape, jnp.float32)

    qi = lax.broadcasted_iota(jnp.int32, (N_BACK, 2 * N_BACK), 0)
    kc = lax.broadcasted_iota(jnp.int32, (N_BACK, 2 * N_BACK), 1)
    steps = N_BACK + qi - kc
    in_band = (steps >= 0) & (steps <= N_BACK)

    refs = ((q0, k0, v0), (q1, k1, v1), (q2, k2, v2))
    for g, (win, dil) in enumerate(DIL_PATTERNS):
        q_ref, k_ref, v_ref = refs[g]
        span = dil * N_BACK
        nb = seq // span
        dist = (steps * dil).astype(jnp.float32)
        slopes = [_slope_like((1, 2 * N_BACK), g, slab * HEADS_PER_SLAB + h)
                  for h in range(HEADS_PER_SLAB)]

        def tile(t, carry, q_ref=q_ref, k_ref=k_ref, v_ref=v_ref, dil=dil,
                 span=span, nb=nb, dist=dist, slopes=slopes):
            r = t // nb
            n = t - r * nb
            start = n * span + r
            prev = jnp.maximum(start - span, r)
            rows = pl.ds(start, N_BACK, stride=dil)
            prows = pl.ds(prev, N_BACK, stride=dil)
            q = q_ref[rows, :] * scale
            kk = jnp.concatenate([k_ref[prows, :], k_ref[rows, :]], axis=0).astype(jnp.bfloat16)
            vv = jnp.concatenate([v_ref[prows, :], v_ref[rows, :]], axis=0).astype(jnp.bfloat16)
            valid = in_band & ((kc >= N_BACK) | (n > 0))
            stats = []
            for h in range(HEADS_PER_SLAB):
                mine = first if h == 0 else jnp.logical_not(first)
                qm = jnp.where(mine, q, 0.0).astype(jnp.bfloat16)
                s = lax.dot_general(qm, kk, _NT, preferred_element_type=jnp.float32)
                s = s + jnp.where(valid, -slopes[h] * dist, NEG_INF)
                m = jnp.max(s, axis=-1, keepdims=True)
                p = jnp.exp(s - m)
                l = jnp.sum(p, axis=-1, keepdims=True)
                o = jnp.dot(p.astype(jnp.bfloat16), vv, preferred_element_type=jnp.float32)
                stats.append((m, l, o))
            m_t = jnp.where(first, stats[0][0], stats[1][0])
            l_t = jnp.where(first, stats[0][1], stats[1][1])
            o_t = jnp.where(first, stats[0][2], stats[1][2])
            m_old = m_acc[rows, :]
            m_new = jnp.maximum(m_old, m_t)
            a_old = jnp.exp(m_old - m_new)
            a_t = jnp.exp(m_t - m_new)
            m_acc[rows, :] = m_new
            l_acc[rows, :] = a_old * l_acc[rows, :] + a_t * l_t
            o_acc[rows, :] = a_old * o_acc[rows, :] + a_t * o_t
            return carry

        lax.fori_loop(0, dil * nb, tile, 0)

    o_ref[...] = o_acc[...] / l_acc[...]


def _prompt_attention(q, k, v, n_seq, seq):
    assert seq % (DIL_PATTERNS[-1][1] * N_BACK) == 0
    in_specs = []
    for g in range(N_GROUPS):
        for _ in range(3):
            in_specs.append(pl.BlockSpec((seq, LANES), lambda b, j, g=g: (b, g * N_SLABS + j)))
    args = []
    for g in range(N_GROUPS):
        args += [q, k, v]
    return pl.pallas_call(
        functools.partial(_prompt_attn_kernel, seq=seq),
        grid=(n_seq, N_SLABS),
        in_specs=in_specs,
        out_specs=pl.BlockSpec((seq, LANES), lambda b, j: (b, j)),
        out_shape=jax.ShapeDtypeStruct((n_seq * seq, ATT_WIDTH), jnp.float32),
        scratch_shapes=[pltpu.VMEM((seq, LANES), jnp.float32)] * 3,
        compiler_params=_params(),
        name="prompt_attention",
    )(*args)


def _sample_attn_kernel(*refs, t_new):
    ins = refs[:5 * N_GROUPS]
    outs = refs[5 * N_GROUPS:5 * N_GROUPS + 2 * N_GROUPS + 1]
    scr = refs[5 * N_GROUPS + 2 * N_GROUPS + 1:]
    o_ref = outs[-1]
    slab = pl.program_id(1)
    lane = lax.broadcasted_iota(jnp.int32, (1, LANES), 1)
    head = slab * HEADS_PER_SLAB + (lane >= HEAD_DIM).astype(jnp.int32)
    scale = HEAD_DIM ** -0.5
    jr = lax.broadcasted_iota(jnp.int32, (LANES, LANES), 0) < HEAD_DIM
    jc = lax.broadcasted_iota(jnp.int32, (LANES, LANES), 1) < HEAD_DIM
    same_head = jnp.where(jr == jc, 1.0, 0.0).astype(jnp.bfloat16)
    back = (N_BACK - lax.broadcasted_iota(jnp.int32, (N_BACK, 1), 0)).astype(jnp.float32)

    m_run = l_run = o_run = None
    for g, (win, dil) in enumerate(DIL_PATTERNS):
        qn_ref, kn_ref, vn_ref, ck_ref, cv_ref = ins[5 * g:5 * g + 5]
        nk_ref, nv_ref = outs[2 * g:2 * g + 2]
        kall, vall = scr[2 * g:2 * g + 2]
        w = ck_ref.shape[0]
        kn = kn_ref[...]
        vn = vn_ref[...]
        kall[0:w, :] = ck_ref[...]
        kall[w:w + t_new, :] = kn
        vall[0:w, :] = cv_ref[...]
        vall[w:w + t_new, :] = vn
        nk_ref[...] = kall[t_new:w + t_new, :]
        nv_ref[...] = vall[t_new:w + t_new, :]

        qs = qn_ref[...] * scale
        bias = -_slope_like((1, LANES), g, head) * (back * float(dil))
        prods = [kall[pl.ds(t, N_BACK, stride=dil), :] * qs[t:t + 1, :] for t in range(t_new)]
        prods.append(kn * qs)
        prod = jnp.concatenate(prods, axis=0)
        hi = prod.astype(jnp.bfloat16)
        lo = (prod - hi.astype(jnp.float32)).astype(jnp.bfloat16)
        sc = (jnp.dot(hi, same_head, preferred_element_type=jnp.float32)
              + jnp.dot(lo, same_head, preferred_element_type=jnp.float32))
        s_self = sc[t_new * N_BACK:, :]
        ms, ls, os_ = [], [], []
        for t in range(t_new):
            s = sc[t * N_BACK:(t + 1) * N_BACK, :] + bias
            ss = s_self[t:t + 1, :]
            m = jnp.maximum(jnp.max(s, axis=0, keepdims=True), ss)
            p = jnp.exp(s - m)
            p_self = jnp.exp(ss - m)
            vt = vall[pl.ds(t, N_BACK, stride=dil), :]
            ms.append(m)
            ls.append(jnp.sum(p, axis=0, keepdims=True) + p_self)
            os_.append(jnp.sum(p * vt, axis=0, keepdims=True) + p_self * vn[t:t + 1, :])
        m_g = jnp.concatenate(ms, axis=0)
        l_g = jnp.concatenate(ls, axis=0)
        o_g = jnp.concatenate(os_, axis=0)
        if m_run is None:
            m_run, l_run, o_run = m_g, l_g, o_g
        else:
            m_new = jnp.maximum(m_run, m_g)
            a_old = jnp.exp(m_run - m_new)
            a_g = jnp.exp(m_g - m_new)
            l_run = a_old * l_run + a_g * l_g
            o_run = a_old * o_run + a_g * o_g
            m_run = m_new
    o_ref[...] = o_run / l_run


def _sample_attention(q, k, v, caches_k, caches_v, n_seq, t_new):
    in_specs, args, out_specs, out_shape, scratch = [], [], [], [], []
    for g in range(N_GROUPS):
        w = caches_k[g].shape[1]
        assert w == DIL_PATTERNS[g][0], "cache holds exactly one window"
        new_spec = pl.BlockSpec((t_new, LANES), lambda b, j, g=g: (b, g * N_SLABS + j))
        cache_spec = pl.BlockSpec((None, w, LANES), lambda b, j: (b, 0, j))
        in_specs += [new_spec, new_spec, new_spec, cache_spec, cache_spec]
        args += [q, k, v, caches_k[g], caches_v[g]]
        out_specs += [cache_spec, cache_spec]
        out_shape += [jax.ShapeDtypeStruct(caches_k[g].shape, jnp.float32)] * 2
        scratch += [pltpu.VMEM((w + t_new, LANES), jnp.float32)] * 2
    out_specs.append(pl.BlockSpec((t_new, LANES), lambda b, j: (b, j)))
    out_shape.append(jax.ShapeDtypeStruct((n_seq * t_new, ATT_WIDTH), jnp.float32))
    res = pl.pallas_call(
        functools.partial(_sample_attn_kernel, t_new=t_new),
        grid=(n_seq, N_SLABS),
        in_specs=in_specs,
        out_specs=out_specs,
        out_shape=out_shape,
        scratch_shapes=scratch,
        compiler_params=_params(),
        name="sample_attention",
    )(*args)
    return res[-1], res[:-1]


def _catnorm_kernel(att_ref, cy_ref, ga_ref, gc_ref, o_ref):
    wa = att_ref.shape[1]
    o_ref[:, 0:wa] = _rms(att_ref[...], ga_ref[...]).astype(o_ref.dtype)
    o_ref[:, wa:] = _rms(cy_ref[...], gc_ref[...]).astype(o_ref.dtype)


def _catnorm(att, cy, g_att, g_conv, tm):
    t, wa = att.shape
    wc = cy.shape[1]
    return pl.pallas_call(
        _catnorm_kernel,
        grid=(t // tm,),
        in_specs=[pl.BlockSpec((tm, wa), lambda i: (i, 0)),
                  pl.BlockSpec((tm, wc), lambda i: (i, 0)),
                  pl.BlockSpec((1, wa), lambda i: (0, 0)),
                  pl.BlockSpec((1, wc), lambda i: (0, 0))],
        out_specs=pl.BlockSpec((tm, wa + wc), lambda i: (i, 0)),
        out_shape=jax.ShapeDtypeStruct((t, wa + wc), jnp.bfloat16),
        compiler_params=_params(),
        name="catnorm",
    )(att, cy, g_att.reshape(1, wa), g_conv.reshape(1, wc))


def _outproj_kernel(a_ref, w_ref, x_ref, g_ref, h_ref, hn_ref):
    h = x_ref[...] + jnp.dot(a_ref[...], w_ref[...], preferred_element_type=jnp.float32)
    h_ref[...] = h
    hn_ref[...] = _rms(h, g_ref[...]).astype(hn_ref.dtype)


def _outproj(cat, w_out, x, g_ffn, tm):
    t, k = cat.shape
    d = w_out.shape[1]
    return pl.pallas_call(
        _outproj_kernel,
        grid=(t // tm,),
        in_specs=[pl.BlockSpec((tm, k), lambda i: (i, 0)),
                  pl.BlockSpec((k, d), lambda i: (0, 0)),
                  pl.BlockSpec((tm, d), lambda i: (i, 0)),
                  pl.BlockSpec((1, d), lambda i: (0, 0))],
        out_specs=[pl.BlockSpec((tm, d), lambda i: (i, 0)),
                   pl.BlockSpec((tm, d), lambda i: (i, 0))],
        out_shape=[jax.ShapeDtypeStruct((t, d), jnp.float32),
                   jax.ShapeDtypeStruct((t, d), jnp.bfloat16)],
        compiler_params=_params(),
        name="outproj",
    )(cat, w_out, x, g_ffn.reshape(1, d))


def _top_values(x, k):
    rows = x.shape[0]
    rowf = lax.broadcasted_iota(jnp.int32, x.shape, 0).astype(jnp.float32)
    slot = lax.broadcasted_iota(jnp.int32, (k, x.shape[1]), 0)

    def step(r, carry):
        x, vals = carry
        m = jnp.max(x, axis=0, keepdims=True)
        first = jnp.min(jnp.where(x == m, rowf, float(rows)), axis=0, keepdims=True)
        vals = jnp.where(slot == r, m, vals)
        x = jnp.where(rowf == first, -jnp.inf, x)
        return x, vals

    _, vals = lax.fori_loop(0, k, step, (x, jnp.zeros((k, x.shape[1]), jnp.float32)))
    return vals


def _route_kernel(q_ref, ka_ref, kb_ref, sa_ref, sb_ref, ea_ref, eb_ref, tau_ref):
    ka = ka_ref[...]
    kb = kb_ref[...]
    tt = q_ref.shape[0]

    def per_head(h, carry):
        c0 = pl.multiple_of(h * PEER_QDIM, PEER_QDIM)
        qa = q_ref[:, pl.ds(c0, PEER_HALF)]
        qb = q_ref[:, pl.ds(c0 + PEER_HALF, PEER_HALF)]
        sa = lax.dot_general(ka, qa, _NT, preferred_element_type=jnp.float32)
        sb = lax.dot_general(kb, qb, _NT, preferred_element_type=jnp.float32)
        va = _top_values(sa, PEER_TOPK)
        vb = _top_values(sb, PEER_TOPK)
        cand = jnp.concatenate([va[i:i + 1, :] + vb for i in range(PEER_TOPK)], axis=0)
        top = _top_values(cand, PEER_TOPK)
        z = jnp.sum(jnp.exp(top - top[0:1, :]), axis=0, keepdims=True)
        sa_ref[h] = sa
        sb_ref[h] = sb
        ea_ref[h] = jnp.exp(sa - va[0:1, :])
        eb_ref[h] = jnp.exp(sb - vb[0:1, :]) / z
        tau_ref[h] = top[PEER_TOPK - 1:PEER_TOPK, :]
        return carry

    lax.fori_loop(0, PEER_HEADS, per_head, 0)


def _route(qp, ka, kb, tt):
    t = qp.shape[0]
    big = pl.BlockSpec((PEER_HEADS, N_KEYS, tt), lambda i: (0, 0, i))
    big_shape = jax.ShapeDtypeStruct((PEER_HEADS, N_KEYS, t), jnp.float32)
    return pl.pallas_call(
        _route_kernel,
        grid=(t // tt,),
        in_specs=[pl.BlockSpec((tt, PEER_HEADS * PEER_QDIM), lambda i: (i, 0)),
                  pl.BlockSpec((N_KEYS, PEER_HALF), lambda i: (0, 0)),
                  pl.BlockSpec((N_KEYS, PEER_HALF), lambda i: (0, 0))],
        out_specs=[big, big, big, big, pl.BlockSpec((PEER_HEADS, 1, tt), lambda i: (0, 0, i))],
        out_shape=[big_shape] * 4 + [jax.ShapeDtypeStruct((PEER_HEADS, 1, t), jnp.float32)],
        compiler_params=_params(),
        name="peer_route",
    )(qp, ka, kb)


def _gelu(x):
    return 0.5 * x * (1.0 + lax.erf(x * math.sqrt(0.5)))


def _peer_kernel(hn_ref, u_ref, v_ref, sa_ref, sb_ref, ea_ref, eb_ref, tau_ref, o_ref,
                 h_scr, w_scr):
    e = pl.program_id(1)
    et, tt = h_scr.shape
    n_a = et // N_KEYS

    @pl.when(e == 0)
    def _():
        o_ref[...] = jnp.zeros(o_ref.shape, o_ref.dtype)

    h_scr[...] = lax.dot_general(u_ref[...], hn_ref[...], _NT,
                                 preferred_element_type=jnp.float32)

    a0 = pl.multiple_of(e * n_a, n_a)

    def per_lane_tile(lt, carry):
        ls = pl.ds(pl.multiple_of(lt * LANES, LANES), LANES)
        sa = [sa_ref[h, pl.ds(a0, n_a), ls] for h in range(PEER_HEADS)]
        ea = [ea_ref[h, pl.ds(a0, n_a), ls] for h in range(PEER_HEADS)]
        tau = [tau_ref[h, :, ls] for h in range(PEER_HEADS)]
        for a in range(n_a):
            rows = pl.ds(a * N_KEYS, N_KEYS)
            acc = jnp.zeros((N_KEYS, LANES), jnp.float32)
            for h in range(PEER_HEADS):
                s = sb_ref[h, :, ls] + sa[h][a:a + 1, :]
                gate = eb_ref[h, :, ls] * ea[h][a:a + 1, :]
                acc = acc + jnp.where(s >= tau[h], gate, 0.0)
            w_scr[rows, ls] = (acc * _gelu(h_scr[rows, ls])).astype(w_scr.dtype)
        return carry

    lax.fori_loop(0, tt // LANES, per_lane_tile, 0)
    o_ref[...] += lax.dot_general(w_scr[...], v_ref[...], _TN,
                                  preferred_element_type=jnp.float32)


def _peer_dense(hn, u, v, sa, sb, ea, eb, tau, tt, et):
    t, d = hn.shape
    n_exp = u.shape[0]
    big = pl.BlockSpec((PEER_HEADS, N_KEYS, tt), lambda i, e: (0, 0, i))
    return pl.pallas_call(
        _peer_kernel,
        grid=(t // tt, n_exp // et),
        in_specs=[pl.BlockSpec((tt, d), lambda i, e: (i, 0)),
                  pl.BlockSpec((et, d), lambda i, e: (e, 0)),
                  pl.BlockSpec((et, d), lambda i, e: (e, 0)),
                  big, big, big, big,
                  pl.BlockSpec((PEER_HEADS, 1, tt), lambda i, e: (0, 0, i))],
        out_specs=pl.BlockSpec((tt, d), lambda i, e: (i, 0)),
        out_shape=jax.ShapeDtypeStruct((t, d), jnp.float32),
        scratch_shapes=[pltpu.VMEM((et, tt), jnp.float32),
                        pltpu.VMEM((et, tt), jnp.bfloat16)],
        compiler_params=_params(),
        name="peer_dense",
    )(hn, u, v, sa, sb, ea, eb, tau)


def _final_kernel(h_ref, p_ref, g_ref, o_ref):
    o_ref[...] = _rms(h_ref[...] + p_ref[...], g_ref[...])


def _final(h, p, g, tm):
    t, d = h.shape
    row = pl.BlockSpec((tm, d), lambda i: (i, 0))
    return pl.pallas_call(
        _final_kernel,
        grid=(t // tm,),
        in_specs=[row, row, pl.BlockSpec((1, d), lambda i: (0, 0))],
        out_specs=row,
        out_shape=jax.ShapeDtypeStruct((t, d), jnp.float32),
        compiler_params=_params(),
        name="final_norm",
    )(h, p, g.reshape(1, d))


def _tile(t, pref):
    while t % pref:
        pref //= 2
    return pref


def _project_in(x2, norm_g, w_in_bf):
    t = x2.shape[0]
    xn = _rmsnorm(x2, norm_g, jnp.bfloat16, _tile(t, 256))
    tm = _tile(t, 512)
    cw = (w_in_bf.shape[1] - 3 * QKV_WIDTH)
    q = _matmul(xn, w_in_bf, 0, QKV_WIDTH, jnp.float32, tm, 1024)
    k = _matmul(xn, w_in_bf, QKV_WIDTH, QKV_WIDTH, jnp.float32, tm, 1024)
    v = _matmul(xn, w_in_bf, 2 * QKV_WIDTH, QKV_WIDTH, jnp.float32, tm, 1024)
    pc = _matmul(xn, w_in_bf, 3 * QKV_WIDTH, cw, jnp.float32, tm, 1024)
    return q, k, v, pc


def _channel_mix(x2, att, cy, lw):
    t = x2.shape[0]
    cat = _catnorm(att, cy, lw["g_att"], lw["g_conv"], _tile(t, 256))
    h, hn = _outproj(cat, lw["w_out"], x2, lw["g_ffn"], _tile(t, 256))
    qp = _matmul(hn, lw["w_pq"], 0, lw["w_pq"].shape[1], jnp.bfloat16, _tile(t, 512), 1024)
    sa, sb, ea, eb, tau = _route(qp, lw["ka"], lw["kb"], _tile(t, 256))
    peer = _peer_dense(hn, lw["u"], lw["v"], sa, sb, ea, eb, tau, _tile(t, 512), 1024)
    return h, peer


def kernel(x_prompt, x_sample, cache_k_g0, cache_v_g0, cache_k_g1, cache_v_g1, cache_k_g2, cache_v_g2, state_conv, norm_mix, w_in, conv_w, norm_att_out, norm_conv_out, w_out, norm_ffn, w_pq, sub_keys_a, sub_keys_b, expert_u, expert_v, norm_final):
    bf = jnp.bfloat16
    depth = w_in.shape[0]
    b, s, d = x_prompt.shape
    bd, td, _ = x_sample.shape
    caches_k = (cache_k_g0, cache_k_g1, cache_k_g2)
    caches_v = (cache_v_g0, cache_v_g1, cache_v_g2)
    conv_c = conv_w.shape[2]

    hp = x_prompt.reshape(b * s, d)
    hs = x_sample.reshape(bd * td, d)
    nk_p = [[] for _ in range(N_GROUPS)]
    nv_p = [[] for _ in range(N_GROUPS)]
    nk_s = [[] for _ in range(N_GROUPS)]
    nv_s = [[] for _ in range(N_GROUPS)]
    nc_p, nc_s = [], []
    for l in range(depth):
        lw = dict(g_att=norm_att_out[l], g_conv=norm_conv_out[l], w_out=w_out[l].astype(bf),
                  g_ffn=norm_ffn[l], w_pq=w_pq[l].astype(bf), ka=sub_keys_a[l].astype(bf),
                  kb=sub_keys_b[l].astype(bf), u=expert_u[l].astype(bf), v=expert_v[l].astype(bf))
        w_in_bf = w_in[l].astype(bf)

        q, k, v, pc = _project_in(hp, norm_mix[l], w_in_bf)
        att = _prompt_attention(q, k, v, b, s)
        cy, ns = _gated_conv(pc, jnp.zeros((b, CONV_K - 1, conv_c), jnp.float32), conv_w[l], b, s, 256)
        k5 = k.reshape(b, s, N_GROUPS, ATT_SLOTS, HEAD_DIM)
        v5 = v.reshape(b, s, N_GROUPS, ATT_SLOTS, HEAD_DIM)
        for g, (win, _) in enumerate(DIL_PATTERNS):
            rows = min(win, s)
            nk_p[g].append(k5[:, s - rows:, g])
            nv_p[g].append(v5[:, s - rows:, g])
        nc_p.append(ns)
        h_res, peer = _channel_mix(hp, att, cy, lw)
        last = l == depth - 1
        if last:
            y_prompt = _final(h_res, peer, norm_final, _tile(b * s, 256))
        else:
            hp = h_res + peer

        q, k, v, pc = _project_in(hs, norm_mix[l], w_in_bf)
        ck = [c[l].reshape(bd, c.shape[2], ATT_WIDTH) for c in caches_k]
        cv = [c[l].reshape(bd, c.shape[2], ATT_WIDTH) for c in caches_v]
        att, new_caches = _sample_attention(q, k, v, ck, cv, bd, td)
        for g in range(N_GROUPS):
            shape5 = (bd, ck[g].shape[1], ATT_SLOTS, HEAD_DIM)
            nk_s[g].append(new_caches[2 * g].reshape(shape5))
            nv_s[g].append(new_caches[2 * g + 1].reshape(shape5))
        cy, ns = _gated_conv(pc, state_conv[l], conv_w[l], bd, td, conv_c)
        nc_s.append(ns)
        h_res, peer = _channel_mix(hs, att, cy, lw)
        if last:
            y_sample = _final(h_res, peer, norm_final, _tile(bd * td, 256))
        else:
            hs = h_res + peer

    return (y_prompt.reshape(b, s, d), y_sample.reshape(bd, td, d),
            jnp.stack(nk_p[0], 0), jnp.stack(nv_p[0], 0),
            jnp.stack(nk_p[1], 0), jnp.stack(nv_p[1], 0),
            jnp.stack(nk_p[2], 0), jnp.stack(nv_p[2], 0),
            jnp.stack(nc_p, 0),
            jnp.stack(nk_s[0], 0), jnp.stack(nv_s[0], 0),
            jnp.stack(nk_s[1], 0), jnp.stack(nv_s[1], 0),
            jnp.stack(nk_s[2], 0), jnp.stack(nv_s[2], 0),
            jnp.stack(nc_s, 0))
```

```python
import functools
import math

import jax
import jax.numpy as jnp
from jax import lax
from jax.experimental import pallas as pl
from jax.experimental.pallas import tpu as pltpu

HEAD_DIM = 64
ATT_SLOTS = 16
DIL_PATTERNS = ((128, 1), (512, 4), (2048, 16))
N_GROUPS = len(DIL_PATTERNS)
ATT_WIDTH = ATT_SLOTS * HEAD_DIM
QKV_WIDTH = N_GROUPS * ATT_WIDTH
CONV_K = 3
N_KEYS = 128
PEER_HEADS = 8
PEER_QDIM = 256
PEER_HALF = PEER_QDIM // 2
PEER_TOPK = 16
NORM_EPS = 1e-6
NEG_INF = -1e30

LANES = 128
HEADS_PER_SLAB = LANES // HEAD_DIM
N_SLABS = ATT_WIDTH // LANES
N_BACK = 128
VMEM_LIMIT = 56 * 1024 * 1024

_NT = (((1,), (1,)), ((), ()))
_TN = (((0,), (0,)), ((), ()))


def _params(vmem=None):
    return pltpu.CompilerParams(vmem_limit_bytes=vmem or VMEM_LIMIT)


def _rms(x, g):
    return x * lax.rsqrt(jnp.mean(x * x, axis=-1, keepdims=True) + NORM_EPS) * g


def _rmsnorm_kernel(x_ref, g_ref, o_ref):
    o_ref[...] = _rms(x_ref[...], g_ref[...]).astype(o_ref.dtype)


def _rmsnorm(x, g, out_dtype, tm):
    t, d = x.shape
    return pl.pallas_call(
        _rmsnorm_kernel,
        grid=(t // tm,),
        in_specs=[pl.BlockSpec((tm, d), lambda i: (i, 0)),
                  pl.BlockSpec((1, d), lambda i: (0, 0))],
        out_specs=pl.BlockSpec((tm, d), lambda i: (i, 0)),
        out_shape=jax.ShapeDtypeStruct((t, d), out_dtype),
        compiler_params=_params(),
        name="rmsnorm",
    )(x, g.reshape(1, d))


def _mm_kernel(a_ref, b_ref, o_ref):
    o_ref[...] = jnp.dot(a_ref[...], b_ref[...],
                         preferred_element_type=jnp.float32).astype(o_ref.dtype)


def _matmul(a, b, col_start, n_cols, out_dtype, tm, tn):
    t, k = a.shape
    off = col_start // tn
    return pl.pallas_call(
        _mm_kernel,
        grid=(t // tm, n_cols // tn),
        in_specs=[pl.BlockSpec((tm, k), lambda i, j: (i, 0)),
                  pl.BlockSpec((k, tn), lambda i, j: (0, j + off))],
        out_specs=pl.BlockSpec((tm, tn), lambda i, j: (i, j)),
        out_shape=jax.ShapeDtypeStruct((t, n_cols), out_dtype),
        compiler_params=_params(),
        name="matmul",
    )(a, b)


def _conv_kernel(gb_ref, gc_ref, h_ref, st_ref, w_ref, y_ref, ns_ref):
    u = gc_ref[...] * h_ref[...]
    s = u.shape[0]
    row = lax.broadcasted_iota(jnp.int32, u.shape, 0)
    st = st_ref[...]
    w = w_ref[...]
    u1 = jnp.where(row == 0, st[1:2, :], pltpu.roll(u, 1, 0))
    u2 = jnp.where(row == 0, st[0:1, :],
                   jnp.where(row == 1, st[1:2, :], pltpu.roll(u, 2, 0)))
    y = u2 * w[0:1, :]
    y = y + u1 * w[1:2, :]
    y = y + u * w[2:3, :]
    y_ref[...] = gb_ref[...] * y
    ns_ref[...] = u[s - 2:s, :]


def _gated_conv(pc, state, conv_w, n_seq, seq, cb):
    c = conv_w.shape[1]
    ncb = c // cb
    return pl.pallas_call(
        _conv_kernel,
        grid=(n_seq, ncb),
        in_specs=[pl.BlockSpec((seq, cb), lambda b, j: (b, j)),
                  pl.BlockSpec((seq, cb), lambda b, j: (b, ncb + j)),
                  pl.BlockSpec((seq, cb), lambda b, j: (b, 2 * ncb + j)),
                  pl.BlockSpec((None, CONV_K - 1, cb), lambda b, j: (b, 0, j)),
                  pl.BlockSpec((CONV_K, cb), lambda b, j: (0, j))],
        out_specs=[pl.BlockSpec((seq, cb), lambda b, j: (b, j)),
                   pl.BlockSpec((None, CONV_K - 1, cb), lambda b, j: (b, 0, j))],
        out_shape=[jax.ShapeDtypeStruct((n_seq * seq, c), jnp.float32),
                   jax.ShapeDtypeStruct((n_seq, CONV_K - 1, c), jnp.float32)],
        compiler_params=_params(),
        name="gated_conv",
    )(pc, pc, pc, state, conv_w)


def _slope_like(shape, group, head):
    i = (group * ATT_SLOTS + 1 + head).astype(jnp.float32)
    n = float(N_GROUPS * ATT_SLOTS)
    return jnp.exp2(jnp.broadcast_to(-8.0 * i / n, shape))


def _prompt_attn_kernel(q0, k0, v0, q1, k1, v1, q2, k2, v2, o_ref, *stat_refs, seq):
    slab = pl.program_id(1)
    lane = lax.broadcasted_iota(jnp.int32, (1, LANES), 1)
    first = lane < HEAD_DIM
    scale = HEAD_DIM ** -0.5

    refs = ((q0, k0, v0), (q1, k1, v1), (q2, k2, v2))
    for g, (win, dil) in enumerate(DIL_PATTERNS):
        q_ref, k_ref, v_ref = refs[g]
        m_ref, l_ref, n_ref = stat_refs[3 * g:3 * g + 3]
        span = dil * N_BACK
        nb = seq // span
        two = nb > 1
        nk = (2 if two else 1) * N_BACK
        qi = lax.broadcasted_iota(jnp.int32, (N_BACK, nk), 0)
        kc = lax.broadcasted_iota(jnp.int32, (N_BACK, nk), 1)
        steps = (N_BACK if two else 0) + qi - kc
        in_band = (steps >= 0) & (steps <= N_BACK)
        dist = (steps * dil).astype(jnp.float32)
        slopes = [_slope_like((1, nk), g, slab * HEADS_PER_SLAB + h)
                  for h in range(HEADS_PER_SLAB)]

        def tile(t, carry, q_ref=q_ref, k_ref=k_ref, v_ref=v_ref, m_ref=m_ref, l_ref=l_ref,
                 n_ref=n_ref, dil=dil, span=span, nb=nb, two=two, kc=kc, in_band=in_band,
                 dist=dist, slopes=slopes):
            r = t // nb
            n = t - r * nb
            start = n * span + r
            rows = pl.ds(start, N_BACK, stride=dil)
            q = q_ref[rows, :] * scale
            if two:
                prev = jnp.maximum(start - span, r)
                prows = pl.ds(prev, N_BACK, stride=dil)
                kk = jnp.concatenate([k_ref[prows, :], k_ref[rows, :]], axis=0)
                vv = jnp.concatenate([v_ref[prows, :], v_ref[rows, :]], axis=0)
                valid = in_band & ((kc >= N_BACK) | (n > 0))
            else:
                kk, vv, valid = k_ref[rows, :], v_ref[rows, :], in_band
            kk = kk.astype(jnp.bfloat16)
            vv = vv.astype(jnp.bfloat16)
            stats = []
            for h in range(HEADS_PER_SLAB):
                mine = first if h == 0 else jnp.logical_not(first)
                qm = jnp.where(mine, q, 0.0).astype(jnp.bfloat16)
                s = lax.dot_general(qm, kk, _NT, preferred_element_type=jnp.float32)
                s = s + jnp.where(valid, -slopes[h] * dist, NEG_INF)
                m = jnp.max(s, axis=-1, keepdims=True)
                p = jnp.exp(s - m)
                l = jnp.sum(p, axis=-1, keepdims=True)
                o = jnp.dot(p.astype(jnp.bfloat16), vv, preferred_element_type=jnp.float32)
                stats.append((m, l, o))
            m_ref[rows, :] = jnp.where(first, stats[0][0], stats[1][0])
            l_ref[rows, :] = jnp.where(first, stats[0][1], stats[1][1])
            n_ref[rows, :] = jnp.where(first, stats[0][2], stats[1][2])
            return carry

        lax.fori_loop(0, dil * nb, tile, 0, unroll=2)

    chunk = 256

    def merge(c, carry):
        rows = pl.ds(pl.multiple_of(c * chunk, chunk), chunk)
        ms = [stat_refs[3 * g][rows, :] for g in range(N_GROUPS)]
        m = functools.reduce(jnp.maximum, ms)
        den = num = None
        for g in range(N_GROUPS):
            a = jnp.exp(ms[g] - m)
            dg = a * stat_refs[3 * g + 1][rows, :]
            ng = a * stat_refs[3 * g + 2][rows, :]
            den = dg if den is None else den + dg
            num = ng if num is None else num + ng
        o_ref[rows, :] = num / den
        return carry

    lax.fori_loop(0, seq // chunk, merge, 0)


def _prompt_attention(q, k, v, n_seq, seq):
    assert seq % (DIL_PATTERNS[-1][1] * N_BACK) == 0
    in_specs = []
    for g in range(N_GROUPS):
        for _ in range(3):
            in_specs.append(pl.BlockSpec((seq, LANES), lambda b, j, g=g: (b, g * N_SLABS + j)))
    args = []
    for g in range(N_GROUPS):
        args += [q, k, v]
    return pl.pallas_call(
        functools.partial(_prompt_attn_kernel, seq=seq),
        grid=(n_seq, N_SLABS),
        in_specs=in_specs,
        out_specs=pl.BlockSpec((seq, LANES), lambda b, j: (b, j)),
        out_shape=jax.ShapeDtypeStruct((n_seq * seq, ATT_WIDTH), jnp.float32),
        scratch_shapes=[pltpu.VMEM((seq, LANES), jnp.float32)] * (3 * N_GROUPS),
        compiler_params=_params(),
        name="prompt_attention",
    )(*args)


def _sample_attn_kernel(*refs, t_new):
    ins = refs[:5 * N_GROUPS]
    outs = refs[5 * N_GROUPS:5 * N_GROUPS + 2 * N_GROUPS + 1]
    scr = refs[5 * N_GROUPS + 2 * N_GROUPS + 1:]
    o_ref = outs[-1]
    slab = pl.program_id(1)
    lane = lax.broadcasted_iota(jnp.int32, (1, LANES), 1)
    head = slab * HEADS_PER_SLAB + (lane >= HEAD_DIM).astype(jnp.int32)
    scale = HEAD_DIM ** -0.5
    jr = lax.broadcasted_iota(jnp.int32, (LANES, LANES), 0) < HEAD_DIM
    jc = lax.broadcasted_iota(jnp.int32, (LANES, LANES), 1) < HEAD_DIM
    same_head = jnp.where(jr == jc, 1.0, 0.0).astype(jnp.bfloat16)
    back = (N_BACK - lax.broadcasted_iota(jnp.int32, (N_BACK, 1), 0)).astype(jnp.float32)

    m_run = l_run = o_run = None
    for g, (win, dil) in enumerate(DIL_PATTERNS):
        qn_ref, kn_ref, vn_ref, ck_ref, cv_ref = ins[5 * g:5 * g + 5]
        nk_ref, nv_ref = outs[2 * g:2 * g + 2]
        kall, vall = scr[2 * g:2 * g + 2]
        w = ck_ref.shape[0]
        kn = kn_ref[...]
        vn = vn_ref[...]
        kall[0:w, :] = ck_ref[...]
        kall[w:w + t_new, :] = kn
        vall[0:w, :] = cv_ref[...]
        vall[w:w + t_new, :] = vn
        nk_ref[...] = kall[t_new:w + t_new, :]
        nv_ref[...] = vall[t_new:w + t_new, :]

        qs = qn_ref[...] * scale
        bias = -_slope_like((1, LANES), g, head) * (back * float(dil))
        prods = [kall[pl.ds(t, N_BACK, stride=dil), :] * qs[t:t + 1, :] for t in range(t_new)]
        prods.append(kn * qs)
        prod = jnp.concatenate(prods, axis=0)
        hi = prod.astype(jnp.bfloat16)
        lo = (prod - hi.astype(jnp.float32)).astype(jnp.bfloat16)
        sc = (jnp.dot(hi, same_head, preferred_element_type=jnp.float32)
              + jnp.dot(lo, same_head, preferred_element_type=jnp.float32))
        s_self = sc[t_new * N_BACK:, :]
        ms, ls, os_ = [], [], []
        for t in range(t_new):
            s = sc[t * N_BACK:(t + 1) * N_BACK, :] + bias
            ss = s_self[t:t + 1, :]
            m = jnp.maximum(jnp.max(s, axis=0, keepdims=True), ss)
            p = jnp.exp(s - m)
            p_self = jnp.exp(ss - m)
            vt = vall[pl.ds(t, N_BACK, stride=dil), :]
            ms.append(m)
            ls.append(jnp.sum(p, axis=0, keepdims=True) + p_self)
            os_.append(jnp.sum(p * vt, axis=0, keepdims=True) + p_self * vn[t:t + 1, :])
        m_g = jnp.concatenate(ms, axis=0)
        l_g = jnp.concatenate(ls, axis=0)
        o_g = jnp.concatenate(os_, axis=0)
        if m_run is None:
            m_run, l_run, o_run = m_g, l_g, o_g
        else:
            m_new = jnp.maximum(m_run, m_g)
            a_old = jnp.exp(m_run - m_new)
            a_g = jnp.exp(m_g - m_new)
            l_run = a_old * l_run + a_g * l_g
            o_run = a_old * o_run + a_g * o_g
            m_run = m_new
    o_ref[...] = o_run / l_run


def _sample_attention(q, k, v, caches_k, caches_v, n_seq, t_new):
    in_specs, args, out_specs, out_shape, scratch = [], [], [], [], []
    for g in range(N_GROUPS):
        w = caches_k[g].shape[1]
        assert w == DIL_PATTERNS[g][0], "cache holds exactly one window"
        new_spec = pl.BlockSpec((t_new, LANES), lambda b, j, g=g: (b, g * N_SLABS + j))
        cache_spec = pl.BlockSpec((None, w, LANES), lambda b, j: (b, 0, j))
        in_specs += [new_spec, new_spec, new_spec, cache_spec, cache_spec]
        args += [q, k, v, caches_k[g], caches_v[g]]
        out_specs += [cache_spec, cache_spec]
        out_shape += [jax.ShapeDtypeStruct(caches_k[g].shape, jnp.float32)] * 2
        scratch += [pltpu.VMEM((w + t_new, LANES), jnp.float32)] * 2
    out_specs.append(pl.BlockSpec((t_new, LANES), lambda b, j: (b, j)))
    out_shape.append(jax.ShapeDtypeStruct((n_seq * t_new, ATT_WIDTH), jnp.float32))
    res = pl.pallas_call(
        functools.partial(_sample_attn_kernel, t_new=t_new),
        grid=(n_seq, N_SLABS),
        in_specs=in_specs,
        out_specs=out_specs,
        out_shape=out_shape,
        scratch_shapes=scratch,
        compiler_params=_params(),
        name="sample_attention",
    )(*args)
    return res[-1], res[:-1]


def _catnorm_kernel(att_ref, cy_ref, ga_ref, gc_ref, o_ref):
    wa = att_ref.shape[1]
    o_ref[:, 0:wa] = _rms(att_ref[...], ga_ref[...]).astype(o_ref.dtype)
    o_ref[:, wa:] = _rms(cy_ref[...], gc_ref[...]).astype(o_ref.dtype)


def _catnorm(att, cy, g_att, g_conv, tm):
    t, wa = att.shape
    wc = cy.shape[1]
    return pl.pallas_call(
        _catnorm_kernel,
        grid=(t // tm,),
        in_specs=[pl.BlockSpec((tm, wa), lambda i: (i, 0)),
                  pl.BlockSpec((tm, wc), lambda i: (i, 0)),
                  pl.BlockSpec((1, wa), lambda i: (0, 0)),
                  pl.BlockSpec((1, wc), lambda i: (0, 0))],
        out_specs=pl.BlockSpec((tm, wa + wc), lambda i: (i, 0)),
        out_shape=jax.ShapeDtypeStruct((t, wa + wc), jnp.bfloat16),
        compiler_params=_params(),
        name="catnorm",
    )(att, cy, g_att.reshape(1, wa), g_conv.reshape(1, wc))


def _outproj_kernel(a_ref, w_ref, x_ref, g_ref, h_ref, hn_ref):
    h = x_ref[...] + jnp.dot(a_ref[...], w_ref[...], preferred_element_type=jnp.float32)
    h_ref[...] = h
    hn_ref[...] = _rms(h, g_ref[...]).astype(hn_ref.dtype)


def _outproj(cat, w_out, x, g_ffn, tm):
    t, k = cat.shape
    d = w_out.shape[1]
    return pl.pallas_call(
        _outproj_kernel,
        grid=(t // tm,),
        in_specs=[pl.BlockSpec((tm, k), lambda i: (i, 0)),
                  pl.BlockSpec((k, d), lambda i: (0, 0)),
                  pl.BlockSpec((tm, d), lambda i: (i, 0)),
                  pl.BlockSpec((1, d), lambda i: (0, 0))],
        out_specs=[pl.BlockSpec((tm, d), lambda i: (i, 0)),
                   pl.BlockSpec((tm, d), lambda i: (i, 0))],
        out_shape=[jax.ShapeDtypeStruct((t, d), jnp.float32),
                   jax.ShapeDtypeStruct((t, d), jnp.bfloat16)],
        compiler_params=_params(),
        name="outproj",
    )(cat, w_out, x, g_ffn.reshape(1, d))


_N_RANK = PEER_TOPK + 1
_CAND_PER_RANK = tuple(_N_RANK // (i + 1) for i in range(_N_RANK))
_N_CAND = sum(_CAND_PER_RANK)
_CAND_ROWS = -(-_N_CAND // 8) * 8


def _top_values(xs, k):
    tt = xs[0].shape[1]
    slot = lax.broadcasted_iota(jnp.int32, (k, tt), 0)
    rowfs = [lax.broadcasted_iota(jnp.int32, x.shape, 0).astype(jnp.float32) for x in xs]

    def step(r, carry):
        new = []
        for (x, vals), rowf in zip(carry, rowfs):
            m = jnp.max(x, axis=0, keepdims=True)
            first = jnp.min(jnp.where(x == m, rowf, float(x.shape[0])), axis=0, keepdims=True)
            new.append((jnp.where(rowf == first, -jnp.inf, x), jnp.where(slot == r, m, vals)))
        return tuple(new)

    init = tuple((x, jnp.zeros((k, tt), jnp.float32)) for x in xs)
    return [vals for _, vals in lax.fori_loop(0, k, step, init)]


def _route_kernel(q_ref, ka_ref, kb_ref, thr_ref, sb_ref, ea_ref, eb_ref):
    ka = ka_ref[...]
    kb = kb_ref[...]
    tt = q_ref.shape[0]
    sub = N_KEYS // 8

    def per_head(h, carry):
        c0 = pl.multiple_of(h * PEER_QDIM, PEER_QDIM)
        qa = q_ref[:, pl.ds(c0, PEER_HALF)]
        qb = q_ref[:, pl.ds(c0 + PEER_HALF, PEER_HALF)]
        sa = lax.dot_general(ka, qa, _NT, preferred_element_type=jnp.float32)
        sb = lax.dot_general(kb, qb, _NT, preferred_element_type=jnp.float32)
        va, vb = _top_values((sa, sb), _N_RANK)
        pieces = [va[i:i + 1, :] + vb[0:n, :] for i, n in enumerate(_CAND_PER_RANK)]
        pieces.append(jnp.full((_CAND_ROWS - _N_CAND, tt), -jnp.inf, jnp.float32))
        (top,) = _top_values((jnp.concatenate(pieces, axis=0),), _N_RANK)
        z = jnp.sum(jnp.exp(top[0:PEER_TOPK, :] - top[0:1, :]), axis=0, keepdims=True)
        tau = 0.5 * (top[PEER_TOPK - 1:PEER_TOPK, :] + top[PEER_TOPK:PEER_TOPK + 1, :])
        thr = tau - sa
        ea = jnp.exp(sa - va[0:1, :])
        eb = jnp.exp(sb - vb[0:1, :]) / z
        for lt in range(tt // LANES):
            ls = slice(lt * LANES, (lt + 1) * LANES)
            thr_ref[h, lt] = thr[:, ls].reshape(sub, 8, LANES)
            ea_ref[h, lt] = ea[:, ls].reshape(sub, 8, LANES)
            sb_ref[h, lt] = sb[:, ls]
            eb_ref[h, lt] = eb[:, ls]
        return carry

    lax.fori_loop(0, PEER_HEADS, per_head, 0)


def _route_specs(tt, imap):
    n_lt = tt // LANES
    row = pl.BlockSpec((PEER_HEADS, n_lt, N_KEYS // 8, 8, LANES), lambda *g: (0, imap(*g), 0, 0, 0))
    full = pl.BlockSpec((PEER_HEADS, n_lt, N_KEYS, LANES), lambda *g: (0, imap(*g), 0, 0))
    return [row, full, row, full]


def _route(qp, ka, kb, tt):
    t = qp.shape[0]
    row_shape = jax.ShapeDtypeStruct((PEER_HEADS, t // LANES, N_KEYS // 8, 8, LANES), jnp.float32)
    full_shape = jax.ShapeDtypeStruct((PEER_HEADS, t // LANES, N_KEYS, LANES), jnp.float32)
    return pl.pallas_call(
        _route_kernel,
        grid=(t // tt,),
        in_specs=[pl.BlockSpec((tt, PEER_HEADS * PEER_QDIM), lambda i: (i, 0)),
                  pl.BlockSpec((N_KEYS, PEER_HALF), lambda i: (0, 0)),
                  pl.BlockSpec((N_KEYS, PEER_HALF), lambda i: (0, 0))],
        out_specs=_route_specs(tt, lambda i: i),
        out_shape=[row_shape, full_shape, row_shape, full_shape],
        compiler_params=_params(),
        name="peer_route",
    )(qp, ka, kb)


def _gelu(x):
    return 0.5 * x * (1.0 + lax.erf(x * math.sqrt(0.5)))


def _peer_kernel(hn_ref, u_ref, v_ref, thr_ref, sb_ref, ea_ref, eb_ref, o_ref, h_scr, w_scr):
    e = pl.program_id(1)
    et, tt = h_scr.shape
    assert et == 8 * N_KEYS

    @pl.when(e == 0)
    def _():
        o_ref[...] = jnp.zeros(o_ref.shape, o_ref.dtype)

    h_scr[...] = lax.dot_general(u_ref[...], hn_ref[...], _NT,
                                 preferred_element_type=jnp.float32)
    for a in range(et // N_KEYS):
        rows = pl.ds(a * N_KEYS, N_KEYS)
        for lt in range(tt // LANES):
            ls = pl.ds(lt * LANES, LANES)
            acc = None
            for h in range(PEER_HEADS):
                keep = sb_ref[h, lt] >= thr_ref[h, lt, e, a:a + 1, :]
                gate = jnp.where(keep, eb_ref[h, lt], 0.0) * ea_ref[h, lt, e, a:a + 1, :]
                acc = gate if acc is None else acc + gate
            w_scr[rows, ls] = (acc * _gelu(h_scr[rows, ls])).astype(w_scr.dtype)
    o_ref[...] += lax.dot_general(w_scr[...], v_ref[...], _TN,
                                  preferred_element_type=jnp.float32)


def _peer_dense(hn, u, v, routing, tt, et):
    t, d = hn.shape
    n_exp = u.shape[0]
    return pl.pallas_call(
        _peer_kernel,
        grid=(t // tt, n_exp // et),
        in_specs=[pl.BlockSpec((tt, d), lambda i, e: (i, 0)),
                  pl.BlockSpec((et, d), lambda i, e: (e, 0)),
                  pl.BlockSpec((et, d), lambda i, e: (e, 0))] + _route_specs(tt, lambda i, e: i),
        out_specs=pl.BlockSpec((tt, d), lambda i, e: (i, 0)),
        out_shape=jax.ShapeDtypeStruct((t, d), jnp.float32),
        scratch_shapes=[pltpu.VMEM((et, tt), jnp.float32),
                        pltpu.VMEM((et, tt), jnp.bfloat16)],
        compiler_params=_params(),
        name="peer_dense",
    )(hn, u, v, *routing)


def _final_kernel(h_ref, p_ref, g_ref, o_ref):
    o_ref[...] = _rms(h_ref[...] + p_ref[...], g_ref[...])


def _final(h, p, g, tm):
    t, d = h.shape
    row = pl.BlockSpec((tm, d), lambda i: (i, 0))
    return pl.pallas_call(
        _final_kernel,
        grid=(t // tm,),
        in_specs=[row, row, pl.BlockSpec((1, d), lambda i: (0, 0))],
        out_specs=row,
        out_shape=jax.ShapeDtypeStruct((t, d), jnp.float32),
        compiler_params=_params(),
        name="final_norm",
    )(h, p, g.reshape(1, d))


def _tile(t, pref):
    while t % pref:
        pref //= 2
    return pref


def _project_in(x2, norm_g, w_in_bf):
    t = x2.shape[0]
    xn = _rmsnorm(x2, norm_g, jnp.bfloat16, _tile(t, 256))
    tm = _tile(t, 1024)
    cw = (w_in_bf.shape[1] - 3 * QKV_WIDTH)
    q = _matmul(xn, w_in_bf, 0, QKV_WIDTH, jnp.float32, tm, 1024)
    k = _matmul(xn, w_in_bf, QKV_WIDTH, QKV_WIDTH, jnp.float32, tm, 1024)
    v = _matmul(xn, w_in_bf, 2 * QKV_WIDTH, QKV_WIDTH, jnp.float32, tm, 1024)
    pc = _matmul(xn, w_in_bf, 3 * QKV_WIDTH, cw, jnp.float32, tm, 1024)
    return q, k, v, pc


def _channel_mix(x2, att, cy, lw):
    t = x2.shape[0]
    cat = _catnorm(att, cy, lw["g_att"], lw["g_conv"], _tile(t, 256))
    h, hn = _outproj(cat, lw["w_out"], x2, lw["g_ffn"], _tile(t, 256))
    qp = _matmul(hn, lw["w_pq"], 0, lw["w_pq"].shape[1], jnp.bfloat16, _tile(t, 1024), 1024)
    routing = _route(qp, lw["ka"], lw["kb"], _tile(t, 256))
    peer = _peer_dense(hn, lw["u"], lw["v"], routing, _tile(t, 512), 8 * N_KEYS)
    return h, peer


def kernel(x_prompt, x_sample, cache_k_g0, cache_v_g0, cache_k_g1, cache_v_g1, cache_k_g2, cache_v_g2, state_conv, norm_mix, w_in, conv_w, norm_att_out, norm_conv_out, w_out, norm_ffn, w_pq, sub_keys_a, sub_keys_b, expert_u, expert_v, norm_final):
    bf = jnp.bfloat16
    depth = w_in.shape[0]
    b, s, d = x_prompt.shape
    bd, td, _ = x_sample.shape
    caches_k = (cache_k_g0, cache_k_g1, cache_k_g2)
    caches_v = (cache_v_g0, cache_v_g1, cache_v_g2)
    conv_c = conv_w.shape[2]

    hp = x_prompt.reshape(b * s, d)
    hs = x_sample.reshape(bd * td, d)
    nk_p = [[] for _ in range(N_GROUPS)]
    nv_p = [[] for _ in range(N_GROUPS)]
    nk_s = [[] for _ in range(N_GROUPS)]
    nv_s = [[] for _ in range(N_GROUPS)]
    nc_p, nc_s = [], []
    for l in range(depth):
        lw = dict(g_att=norm_att_out[l], g_conv=norm_conv_out[l], w_out=w_out[l].astype(bf),
                  g_ffn=norm_ffn[l], w_pq=w_pq[l].astype(bf), ka=sub_keys_a[l].astype(bf),
                  kb=sub_keys_b[l].astype(bf), u=expert_u[l].astype(bf), v=expert_v[l].astype(bf))
        w_in_bf = w_in[l].astype(bf)

        q, k, v, pc = _project_in(hp, norm_mix[l], w_in_bf)
        att = _prompt_attention(q, k, v, b, s)
        cy, ns = _gated_conv(pc, jnp.zeros((b, CONV_K - 1, conv_c), jnp.float32), conv_w[l], b, s, 256)
        k5 = k.reshape(b, s, N_GROUPS, ATT_SLOTS, HEAD_DIM)
        v5 = v.reshape(b, s, N_GROUPS, ATT_SLOTS, HEAD_DIM)
        for g, (win, _) in enumerate(DIL_PATTERNS):
            rows = min(win, s)
            nk_p[g].append(k5[:, s - rows:, g])
            nv_p[g].append(v5[:, s - rows:, g])
        nc_p.append(ns)
        h_res, peer = _channel_mix(hp, att, cy, lw)
        last = l == depth - 1
        if last:
            y_prompt = _final(h_res, peer, norm_final, _tile(b * s, 256))
        else:
            hp = h_res + peer

        q, k, v, pc = _project_in(hs, norm_mix[l], w_in_bf)
        ck = [c[l].reshape(bd, c.shape[2], ATT_WIDTH) for c in caches_k]
        cv = [c[l].reshape(bd, c.shape[2], ATT_WIDTH) for c in caches_v]
        att, new_caches = _sample_attention(q, k, v, ck, cv, bd, td)
        for g in range(N_GROUPS):
            shape5 = (bd, ck[g].shape[1], ATT_SLOTS, HEAD_DIM)
            nk_s[g].append(new_caches[2 * g].reshape(shape5))
            nv_s[g].append(new_caches[2 * g + 1].reshape(shape5))
        cy, ns = _gated_conv(pc, state_conv[l], conv_w[l], bd, td, conv_c)
        nc_s.append(ns)
        h_res, peer = _channel_mix(hs, att, cy, lw)
        if last:
            y_sample = _final(h_res, peer, norm_final, _tile(bd * td, 256))
        else:
            hs = h_res + peer

    return (y_prompt.reshape(b, s, d), y_sample.reshape(bd, td, d),
            jnp.stack(nk_p[0], 0), jnp.stack(nv_p[0], 0),
            jnp.stack(nk_p[1], 0), jnp.stack(nv_p[1], 0),
            jnp.stack(nk_p[2], 0), jnp.stack(nv_p[2], 0),
            jnp.stack(nc_p, 0),
            jnp.stack(nk_s[0], 0), jnp.stack(nv_s[0], 0),
            jnp.stack(nk_s[1], 0), jnp.stack(nv_s[1], 0),
            jnp.stack(nk_s[2], 0), jnp.stack(nv_s[2], 0),
            jnp.stack(nc_s, 0))
```

```python
import functools
import math

import jax
import jax.numpy as jnp
from jax import lax
from jax.experimental import pallas as pl
from jax.experimental.pallas import tpu as pltpu

HEAD_DIM = 64
ATT_SLOTS = 16
DIL_PATTERNS = ((128, 1), (512, 4), (2048, 16))
N_GROUPS = len(DIL_PATTERNS)
ATT_WIDTH = ATT_SLOTS * HEAD_DIM
QKV_WIDTH = N_GROUPS * ATT_WIDTH
CONV_K = 3
N_KEYS = 128
PEER_HEADS = 8
PEER_QDIM = 256
PEER_HALF = PEER_QDIM // 2
PEER_TOPK = 16
NORM_EPS = 1e-6
NEG_INF = -1e30

LANES = 128
HEADS_PER_SLAB = LANES // HEAD_DIM
N_SLABS = ATT_WIDTH // LANES
N_BACK = 128
VMEM_LIMIT = 56 * 1024 * 1024

_NT = (((1,), (1,)), ((), ()))
_TN = (((0,), (0,)), ((), ()))


def _params(vmem=None):
    return pltpu.CompilerParams(vmem_limit_bytes=vmem or VMEM_LIMIT)


def _rms(x, g):
    return x * lax.rsqrt(jnp.mean(x * x, axis=-1, keepdims=True) + NORM_EPS) * g


def _rmsnorm_kernel(x_ref, g_ref, o_ref):
    o_ref[...] = _rms(x_ref[...], g_ref[...]).astype(o_ref.dtype)


def _rmsnorm(x, g, out_dtype, tm):
    t, d = x.shape
    return pl.pallas_call(
        _rmsnorm_kernel,
        grid=(t // tm,),
        in_specs=[pl.BlockSpec((tm, d), lambda i: (i, 0)),
                  pl.BlockSpec((1, d), lambda i: (0, 0))],
        out_specs=pl.BlockSpec((tm, d), lambda i: (i, 0)),
        out_shape=jax.ShapeDtypeStruct((t, d), out_dtype),
        compiler_params=_params(),
        name="rmsnorm",
    )(x, g.reshape(1, d))


def _mm_kernel(a_ref, b_ref, o_ref):
    o_ref[...] = jnp.dot(a_ref[...], b_ref[...],
                         preferred_element_type=jnp.float32).astype(o_ref.dtype)


def _matmul(a, b, col_start, n_cols, out_dtype, tm, tn):
    t, k = a.shape
    off = col_start // tn
    return pl.pallas_call(
        _mm_kernel,
        grid=(t // tm, n_cols // tn),
        in_specs=[pl.BlockSpec((tm, k), lambda i, j: (i, 0)),
                  pl.BlockSpec((k, tn), lambda i, j: (0, j + off))],
        out_specs=pl.BlockSpec((tm, tn), lambda i, j: (i, j)),
        out_shape=jax.ShapeDtypeStruct((t, n_cols), out_dtype),
        compiler_params=_params(),
        name="matmul",
    )(a, b)


def _conv_kernel(gb_ref, gc_ref, h_ref, st_ref, w_ref, y_ref, ns_ref):
    u = gc_ref[...] * h_ref[...]
    s = u.shape[0]
    row = lax.broadcasted_iota(jnp.int32, u.shape, 0)
    st = st_ref[...]
    w = w_ref[...]
    u1 = jnp.where(row == 0, st[1:2, :], pltpu.roll(u, 1, 0))
    u2 = jnp.where(row == 0, st[0:1, :],
                   jnp.where(row == 1, st[1:2, :], pltpu.roll(u, 2, 0)))
    y = u2 * w[0:1, :]
    y = y + u1 * w[1:2, :]
    y = y + u * w[2:3, :]
    y_ref[...] = gb_ref[...] * y
    ns_ref[...] = u[s - 2:s, :]


def _gated_conv(pc, state, conv_w, n_seq, seq, cb):
    c = conv_w.shape[1]
    ncb = c // cb
    return pl.pallas_call(
        _conv_kernel,
        grid=(n_seq, ncb),
        in_specs=[pl.BlockSpec((seq, cb), lambda b, j: (b, j)),
                  pl.BlockSpec((seq, cb), lambda b, j: (b, ncb + j)),
                  pl.BlockSpec((seq, cb), lambda b, j: (b, 2 * ncb + j)),
                  pl.BlockSpec((None, CONV_K - 1, cb), lambda b, j: (b, 0, j)),
                  pl.BlockSpec((CONV_K, cb), lambda b, j: (0, j))],
        out_specs=[pl.BlockSpec((seq, cb), lambda b, j: (b, j)),
                   pl.BlockSpec((None, CONV_K - 1, cb), lambda b, j: (b, 0, j))],
        out_shape=[jax.ShapeDtypeStruct((n_seq * seq, c), jnp.float32),
                   jax.ShapeDtypeStruct((n_seq, CONV_K - 1, c), jnp.float32)],
        compiler_params=_params(),
        name="gated_conv",
    )(pc, pc, pc, state, conv_w)


def _slope_like(shape, group, head):
    i = (group * ATT_SLOTS + 1 + head).astype(jnp.float32)
    n = float(N_GROUPS * ATT_SLOTS)
    return jnp.exp2(jnp.broadcast_to(-8.0 * i / n, shape))


def _prompt_attn_kernel(q0, k0, v0, q1, k1, v1, q2, k2, v2, o_ref, *stat_refs, seq):
    slab = pl.program_id(1)
    lane = lax.broadcasted_iota(jnp.int32, (1, LANES), 1)
    first = lane < HEAD_DIM
    scale = HEAD_DIM ** -0.5

    refs = ((q0, k0, v0), (q1, k1, v1), (q2, k2, v2))
    for g, (win, dil) in enumerate(DIL_PATTERNS):
        q_ref, k_ref, v_ref = refs[g]
        m_ref, l_ref, n_ref = stat_refs[3 * g:3 * g + 3]
        span = dil * N_BACK
        nb = seq // span
        two = nb > 1
        nk = (2 if two else 1) * N_BACK
        qi = lax.broadcasted_iota(jnp.int32, (N_BACK, nk), 0)
        kc = lax.broadcasted_iota(jnp.int32, (N_BACK, nk), 1)
        steps = (N_BACK if two else 0) + qi - kc
        in_band = (steps >= 0) & (steps <= N_BACK)
        dist = (steps * dil).astype(jnp.float32)
        slopes = [_slope_like((1, nk), g, slab * HEADS_PER_SLAB + h)
                  for h in range(HEADS_PER_SLAB)]

        def tile(t, carry, q_ref=q_ref, k_ref=k_ref, v_ref=v_ref, m_ref=m_ref, l_ref=l_ref,
                 n_ref=n_ref, dil=dil, span=span, nb=nb, two=two, kc=kc, in_band=in_band,
                 dist=dist, slopes=slopes):
            r = t // nb
            n = t - r * nb
            start = n * span + r
            rows = pl.ds(start, N_BACK, stride=dil)
            q = q_ref[rows, :] * scale
            if two:
                prev = jnp.maximum(start - span, r)
                prows = pl.ds(prev, N_BACK, stride=dil)
                kk = jnp.concatenate([k_ref[prows, :], k_ref[rows, :]], axis=0)
                vv = jnp.concatenate([v_ref[prows, :], v_ref[rows, :]], axis=0)
                valid = in_band & ((kc >= N_BACK) | (n > 0))
            else:
                kk, vv, valid = k_ref[rows, :], v_ref[rows, :], in_band
            kk = kk.astype(jnp.bfloat16)
            vv = vv.astype(jnp.bfloat16)
            stats = []
            for h in range(HEADS_PER_SLAB):
                mine = first if h == 0 else jnp.logical_not(first)
                qm = jnp.where(mine, q, 0.0).astype(jnp.bfloat16)
                s = lax.dot_general(qm, kk, _NT, preferred_element_type=jnp.float32)
                s = s + jnp.where(valid, -slopes[h] * dist, NEG_INF)
                m = jnp.max(s, axis=-1, keepdims=True)
                p = jnp.exp(s - m)
                l = jnp.sum(p, axis=-1, keepdims=True)
                o = jnp.dot(p.astype(jnp.bfloat16), vv, preferred_element_type=jnp.float32)
                stats.append((m, l, o))
            m_ref[rows, :] = jnp.where(first, stats[0][0], stats[1][0])
            l_ref[rows, :] = jnp.where(first, stats[0][1], stats[1][1])
            n_ref[rows, :] = jnp.where(first, stats[0][2], stats[1][2])
            return carry

        lax.fori_loop(0, dil * nb, tile, 0, unroll=4)

    chunk = 256

    def merge(c, carry):
        rows = pl.ds(pl.multiple_of(c * chunk, chunk), chunk)
        ms = [stat_refs[3 * g][rows, :] for g in range(N_GROUPS)]
        m = functools.reduce(jnp.maximum, ms)
        den = num = None
        for g in range(N_GROUPS):
            a = jnp.exp(ms[g] - m)
            dg = a * stat_refs[3 * g + 1][rows, :]
            ng = a * stat_refs[3 * g + 2][rows, :]
            den = dg if den is None else den + dg
            num = ng if num is None else num + ng
        o_ref[rows, :] = num / den
        return carry

    lax.fori_loop(0, seq // chunk, merge, 0)


def _prompt_attention(q, k, v, n_seq, seq):
    assert seq % (DIL_PATTERNS[-1][1] * N_BACK) == 0
    in_specs = []
    for g in range(N_GROUPS):
        for _ in range(3):
            in_specs.append(pl.BlockSpec((seq, LANES), lambda b, j, g=g: (b, g * N_SLABS + j)))
    args = []
    for g in range(N_GROUPS):
        args += [q, k, v]
    return pl.pallas_call(
        functools.partial(_prompt_attn_kernel, seq=seq),
        grid=(n_seq, N_SLABS),
        in_specs=in_specs,
        out_specs=pl.BlockSpec((seq, LANES), lambda b, j: (b, j)),
        out_shape=jax.ShapeDtypeStruct((n_seq * seq, ATT_WIDTH), jnp.float32),
        scratch_shapes=[pltpu.VMEM((seq, LANES), jnp.float32)] * (3 * N_GROUPS),
        compiler_params=_params(),
        name="prompt_attention",
    )(*args)


def _sample_attn_kernel(*refs, t_new):
    ins = refs[:5 * N_GROUPS]
    outs = refs[5 * N_GROUPS:5 * N_GROUPS + 2 * N_GROUPS + 1]
    scr = refs[5 * N_GROUPS + 2 * N_GROUPS + 1:]
    o_ref = outs[-1]
    slab = pl.program_id(1)
    lane = lax.broadcasted_iota(jnp.int32, (1, LANES), 1)
    head = slab * HEADS_PER_SLAB + (lane >= HEAD_DIM).astype(jnp.int32)
    scale = HEAD_DIM ** -0.5
    jr = lax.broadcasted_iota(jnp.int32, (LANES, LANES), 0) < HEAD_DIM
    jc = lax.broadcasted_iota(jnp.int32, (LANES, LANES), 1) < HEAD_DIM
    same_head = jnp.where(jr == jc, 1.0, 0.0).astype(jnp.bfloat16)
    back = (N_BACK - lax.broadcasted_iota(jnp.int32, (N_BACK, 1), 0)).astype(jnp.float32)

    m_run = l_run = o_run = None
    for g, (win, dil) in enumerate(DIL_PATTERNS):
        qn_ref, kn_ref, vn_ref, ck_ref, cv_ref = ins[5 * g:5 * g + 5]
        nk_ref, nv_ref = outs[2 * g:2 * g + 2]
        kall, vall = scr[2 * g:2 * g + 2]
        w = ck_ref.shape[0]
        kn = kn_ref[...]
        vn = vn_ref[...]
        kall[0:w, :] = ck_ref[...]
        kall[w:w + t_new, :] = kn
        vall[0:w, :] = cv_ref[...]
        vall[w:w + t_new, :] = vn
        nk_ref[...] = kall[t_new:w + t_new, :]
        nv_ref[...] = vall[t_new:w + t_new, :]

        qs = qn_ref[...] * scale
        bias = -_slope_like((1, LANES), g, head) * (back * float(dil))
        prods = [kall[pl.ds(t, N_BACK, stride=dil), :] * qs[t:t + 1, :] for t in range(t_new)]
        prods.append(kn * qs)
        prod = jnp.concatenate(prods, axis=0)
        hi = prod.astype(jnp.bfloat16)
        lo = (prod - hi.astype(jnp.float32)).astype(jnp.bfloat16)
        sc = (jnp.dot(hi, same_head, preferred_element_type=jnp.float32)
              + jnp.dot(lo, same_head, preferred_element_type=jnp.float32))
        s_self = sc[t_new * N_BACK:, :]
        ms, ls, os_ = [], [], []
        for t in range(t_new):
            s = sc[t * N_BACK:(t + 1) * N_BACK, :] + bias
            ss = s_self[t:t + 1, :]
            m = jnp.maximum(jnp.max(s, axis=0, keepdims=True), ss)
            p = jnp.exp(s - m)
            p_self = jnp.exp(ss - m)
            vt = vall[pl.ds(t, N_BACK, stride=dil), :]
            ms.append(m)
            ls.append(jnp.sum(p, axis=0, keepdims=True) + p_self)
            os_.append(jnp.sum(p * vt, axis=0, keepdims=True) + p_self * vn[t:t + 1, :])
        m_g = jnp.concatenate(ms, axis=0)
        l_g = jnp.concatenate(ls, axis=0)
        o_g = jnp.concatenate(os_, axis=0)
        if m_run is None:
            m_run, l_run, o_run = m_g, l_g, o_g
        else:
            m_new = jnp.maximum(m_run, m_g)
            a_old = jnp.exp(m_run - m_new)
            a_g = jnp.exp(m_g - m_new)
            l_run = a_old * l_run + a_g * l_g
            o_run = a_old * o_run + a_g * o_g
            m_run = m_new
    o_ref[...] = o_run / l_run


def _sample_attention(q, k, v, caches_k, caches_v, n_seq, t_new):
    in_specs, args, out_specs, out_shape, scratch = [], [], [], [], []
    for g in range(N_GROUPS):
        w = caches_k[g].shape[1]
        assert w == DIL_PATTERNS[g][0], "cache holds exactly one window"
        new_spec = pl.BlockSpec((t_new, LANES), lambda b, j, g=g: (b, g * N_SLABS + j))
        cache_spec = pl.BlockSpec((None, w, LANES), lambda b, j: (b, 0, j))
        in_specs += [new_spec, new_spec, new_spec, cache_spec, cache_spec]
        args += [q, k, v, caches_k[g], caches_v[g]]
        out_specs += [cache_spec, cache_spec]
        out_shape += [jax.ShapeDtypeStruct(caches_k[g].shape, jnp.float32)] * 2
        scratch += [pltpu.VMEM((w + t_new, LANES), jnp.float32)] * 2
    out_specs.append(pl.BlockSpec((t_new, LANES), lambda b, j: (b, j)))
    out_shape.append(jax.ShapeDtypeStruct((n_seq * t_new, ATT_WIDTH), jnp.float32))
    res = pl.pallas_call(
        functools.partial(_sample_attn_kernel, t_new=t_new),
        grid=(n_seq, N_SLABS),
        in_specs=in_specs,
        out_specs=out_specs,
        out_shape=out_shape,
        scratch_shapes=scratch,
        compiler_params=_params(),
        name="sample_attention",
    )(*args)
    return res[-1], res[:-1]


def _catnorm_kernel(att_ref, cy_ref, ga_ref, gc_ref, o_ref):
    wa = att_ref.shape[1]
    o_ref[:, 0:wa] = _rms(att_ref[...], ga_ref[...]).astype(o_ref.dtype)
    o_ref[:, wa:] = _rms(cy_ref[...], gc_ref[...]).astype(o_ref.dtype)


def _catnorm(att, cy, g_att, g_conv, tm):
    t, wa = att.shape
    wc = cy.shape[1]
    return pl.pallas_call(
        _catnorm_kernel,
        grid=(t // tm,),
        in_specs=[pl.BlockSpec((tm, wa), lambda i: (i, 0)),
                  pl.BlockSpec((tm, wc), lambda i: (i, 0)),
                  pl.BlockSpec((1, wa), lambda i: (0, 0)),
                  pl.BlockSpec((1, wc), lambda i: (0, 0))],
        out_specs=pl.BlockSpec((tm, wa + wc), lambda i: (i, 0)),
        out_shape=jax.ShapeDtypeStruct((t, wa + wc), jnp.bfloat16),
        compiler_params=_params(),
        name="catnorm",
    )(att, cy, g_att.reshape(1, wa), g_conv.reshape(1, wc))


def _outproj_kernel(a_ref, w_ref, x_ref, g_ref, h_ref, hn_ref):
    h = x_ref[...] + jnp.dot(a_ref[...], w_ref[...], preferred_element_type=jnp.float32)
    h_ref[...] = h
    hn_ref[...] = _rms(h, g_ref[...]).astype(hn_ref.dtype)


def _outproj(cat, w_out, x, g_ffn, tm):
    t, k = cat.shape
    d = w_out.shape[1]
    return pl.pallas_call(
        _outproj_kernel,
        grid=(t // tm,),
        in_specs=[pl.BlockSpec((tm, k), lambda i: (i, 0)),
                  pl.BlockSpec((k, d), lambda i: (0, 0)),
                  pl.BlockSpec((tm, d), lambda i: (i, 0)),
                  pl.BlockSpec((1, d), lambda i: (0, 0))],
        out_specs=[pl.BlockSpec((tm, d), lambda i: (i, 0)),
                   pl.BlockSpec((tm, d), lambda i: (i, 0))],
        out_shape=[jax.ShapeDtypeStruct((t, d), jnp.float32),
                   jax.ShapeDtypeStruct((t, d), jnp.bfloat16)],
        compiler_params=_params(),
        name="outproj",
    )(cat, w_out, x, g_ffn.reshape(1, d))


_N_RANK = PEER_TOPK + 1
_CAND_PER_RANK = tuple(_N_RANK // (i + 1) for i in range(_N_RANK))
_N_CAND = sum(_CAND_PER_RANK)
_CAND_ROWS = -(-_N_CAND // 8) * 8


def _top_values(xs, k):
    tt = xs[0].shape[1]
    slot = lax.broadcasted_iota(jnp.int32, (k, tt), 0)
    rowfs = [lax.broadcasted_iota(jnp.int32, x.shape, 0).astype(jnp.float32) for x in xs]

    def step(r, carry):
        new = []
        for (x, vals), rowf in zip(carry, rowfs):
            m = jnp.max(x, axis=0, keepdims=True)
            first = jnp.min(jnp.where(x == m, rowf, float(x.shape[0])), axis=0, keepdims=True)
            new.append((jnp.where(rowf == first, -jnp.inf, x), jnp.where(slot == r, m, vals)))
        return tuple(new)

    init = tuple((x, jnp.zeros((k, tt), jnp.float32)) for x in xs)
    return [vals for _, vals in lax.fori_loop(0, k, step, init)]


def _route_kernel(q_ref, ka_ref, kb_ref, thr_ref, sb_ref, ea_ref, eb_ref):
    ka = ka_ref[...]
    kb = kb_ref[...]
    tt = q_ref.shape[0]
    sub = N_KEYS // 8

    def per_head(h, carry):
        c0 = pl.multiple_of(h * PEER_QDIM, PEER_QDIM)
        qa = q_ref[:, pl.ds(c0, PEER_HALF)]
        qb = q_ref[:, pl.ds(c0 + PEER_HALF, PEER_HALF)]
        sa = lax.dot_general(ka, qa, _NT, preferred_element_type=jnp.float32)
        sb = lax.dot_general(kb, qb, _NT, preferred_element_type=jnp.float32)
        va, vb = _top_values((sa, sb), _N_RANK)
        pieces = [va[i:i + 1, :] + vb[0:n, :] for i, n in enumerate(_CAND_PER_RANK)]
        pieces.append(jnp.full((_CAND_ROWS - _N_CAND, tt), -jnp.inf, jnp.float32))
        (top,) = _top_values((jnp.concatenate(pieces, axis=0),), _N_RANK)
        z = jnp.sum(jnp.exp(top[0:PEER_TOPK, :] - top[0:1, :]), axis=0, keepdims=True)
        tau = 0.5 * (top[PEER_TOPK - 1:PEER_TOPK, :] + top[PEER_TOPK:PEER_TOPK + 1, :])
        thr = tau - sa
        ea = jnp.exp(sa - va[0:1, :])
        eb = jnp.exp(sb - vb[0:1, :]) / z
        for lt in range(tt // LANES):
            ls = slice(lt * LANES, (lt + 1) * LANES)
            thr_ref[h, lt] = thr[:, ls].reshape(sub, 8, LANES)
            ea_ref[h, lt] = ea[:, ls].reshape(sub, 8, LANES)
            sb_ref[h, lt] = sb[:, ls]
            eb_ref[h, lt] = eb[:, ls]
        return carry

    lax.fori_loop(0, PEER_HEADS, per_head, 0)


def _route_specs(tt, imap):
    n_lt = tt // LANES
    row = pl.BlockSpec((PEER_HEADS, n_lt, N_KEYS // 8, 8, LANES), lambda *g: (0, imap(*g), 0, 0, 0))
    full = pl.BlockSpec((PEER_HEADS, n_lt, N_KEYS, LANES), lambda *g: (0, imap(*g), 0, 0))
    return [row, full, row, full]


def _route(qp, ka, kb, tt):
    t = qp.shape[0]
    row_shape = jax.ShapeDtypeStruct((PEER_HEADS, t // LANES, N_KEYS // 8, 8, LANES), jnp.float32)
    full_shape = jax.ShapeDtypeStruct((PEER_HEADS, t // LANES, N_KEYS, LANES), jnp.float32)
    return pl.pallas_call(
        _route_kernel,
        grid=(t // tt,),
        in_specs=[pl.BlockSpec((tt, PEER_HEADS * PEER_QDIM), lambda i: (i, 0)),
                  pl.BlockSpec((N_KEYS, PEER_HALF), lambda i: (0, 0)),
                  pl.BlockSpec((N_KEYS, PEER_HALF), lambda i: (0, 0))],
        out_specs=_route_specs(tt, lambda i: i),
        out_shape=[row_shape, full_shape, row_shape, full_shape],
        compiler_params=_params(),
        name="peer_route",
    )(qp, ka, kb)


def _gelu(x):
    return 0.5 * x * (1.0 + lax.erf(x * math.sqrt(0.5)))


def _peer_kernel(hn_ref, u_ref, v_ref, thr_ref, sb_ref, ea_ref, eb_ref, o_ref, h_scr, w_scr):
    e = pl.program_id(1)
    et, tt = h_scr.shape
    assert et == 8 * N_KEYS

    @pl.when(e == 0)
    def _():
        o_ref[...] = jnp.zeros(o_ref.shape, o_ref.dtype)

    h_scr[...] = lax.dot_general(u_ref[...], hn_ref[...], _NT,
                                 preferred_element_type=jnp.float32)

    def per_lane_tile(lt, carry):
        ls = pl.ds(pl.multiple_of(lt * LANES, LANES), LANES)
        for a in range(et // N_KEYS):
            rows = pl.ds(a * N_KEYS, N_KEYS)
            acc = None
            for h in range(PEER_HEADS):
                keep = sb_ref[h, lt] >= thr_ref[h, lt, e, a:a + 1, :]
                gate = jnp.where(keep, eb_ref[h, lt], 0.0) * ea_ref[h, lt, e, a:a + 1, :]
                acc = gate if acc is None else acc + gate
            w_scr[rows, ls] = (acc * _gelu(h_scr[rows, ls])).astype(w_scr.dtype)
        return carry

    lax.fori_loop(0, tt // LANES, per_lane_tile, 0)
    o_ref[...] += lax.dot_general(w_scr[...], v_ref[...], _TN,
                                  preferred_element_type=jnp.float32)


def _peer_dense(hn, u, v, routing, tt, et):
    t, d = hn.shape
    n_exp = u.shape[0]
    return pl.pallas_call(
        _peer_kernel,
        grid=(t // tt, n_exp // et),
        in_specs=[pl.BlockSpec((tt, d), lambda i, e: (i, 0)),
                  pl.BlockSpec((et, d), lambda i, e: (e, 0)),
                  pl.BlockSpec((et, d), lambda i, e: (e, 0))] + _route_specs(tt, lambda i, e: i),
        out_specs=pl.BlockSpec((tt, d), lambda i, e: (i, 0)),
        out_shape=jax.ShapeDtypeStruct((t, d), jnp.float32),
        scratch_shapes=[pltpu.VMEM((et, tt), jnp.float32),
                        pltpu.VMEM((et, tt), jnp.bfloat16)],
        compiler_params=_params(),
        name="peer_dense",
    )(hn, u, v, *routing)


def _final_kernel(h_ref, p_ref, g_ref, o_ref):
    o_ref[...] = _rms(h_ref[...] + p_ref[...], g_ref[...])


def _final(h, p, g, tm):
    t, d = h.shape
    row = pl.BlockSpec((tm, d), lambda i: (i, 0))
    return pl.pallas_call(
        _final_kernel,
        grid=(t // tm,),
        in_specs=[row, row, pl.BlockSpec((1, d), lambda i: (0, 0))],
        out_specs=row,
        out_shape=jax.ShapeDtypeStruct((t, d), jnp.float32),
        compiler_params=_params(),
        name="final_norm",
    )(h, p, g.reshape(1, d))


def _tile(t, pref):
    while t % pref:
        pref //= 2
    return pref


def _project_in(x2, norm_g, w_in_bf):
    t = x2.shape[0]
    xn = _rmsnorm(x2, norm_g, jnp.bfloat16, _tile(t, 256))
    tm = _tile(t, 1024)
    cw = (w_in_bf.shape[1] - 3 * QKV_WIDTH)
    q = _matmul(xn, w_in_bf, 0, QKV_WIDTH, jnp.float32, tm, 1024)
    k = _matmul(xn, w_in_bf, QKV_WIDTH, QKV_WIDTH, jnp.float32, tm, 1024)
    v = _matmul(xn, w_in_bf, 2 * QKV_WIDTH, QKV_WIDTH, jnp.float32, tm, 1024)
    pc = _matmul(xn, w_in_bf, 3 * QKV_WIDTH, cw, jnp.float32, tm, 1024)
    return q, k, v, pc


def _channel_mix(x2, att, cy, lw):
    t = x2.shape[0]
    cat = _catnorm(att, cy, lw["g_att"], lw["g_conv"], _tile(t, 256))
    h, hn = _outproj(cat, lw["w_out"], x2, lw["g_ffn"], _tile(t, 256))
    qp = _matmul(hn, lw["w_pq"], 0, lw["w_pq"].shape[1], jnp.bfloat16, _tile(t, 1024), 1024)
    routing = _route(qp, lw["ka"], lw["kb"], _tile(t, 256))
    peer = _peer_dense(hn, lw["u"], lw["v"], routing, _tile(t, 512), 8 * N_KEYS)
    return h, peer


def kernel(x_prompt, x_sample, cache_k_g0, cache_v_g0, cache_k_g1, cache_v_g1, cache_k_g2, cache_v_g2, state_conv, norm_mix, w_in, conv_w, norm_att_out, norm_conv_out, w_out, norm_ffn, w_pq, sub_keys_a, sub_keys_b, expert_u, expert_v, norm_final):
    bf = jnp.bfloat16
    depth = w_in.shape[0]
    b, s, d = x_prompt.shape
    bd, td, _ = x_sample.shape
    caches_k = (cache_k_g0, cache_k_g1, cache_k_g2)
    caches_v = (cache_v_g0, cache_v_g1, cache_v_g2)
    conv_c = conv_w.shape[2]

    hp = x_prompt.reshape(b * s, d)
    hs = x_sample.reshape(bd * td, d)
    nk_p = [[] for _ in range(N_GROUPS)]
    nv_p = [[] for _ in range(N_GROUPS)]
    nk_s = [[] for _ in range(N_GROUPS)]
    nv_s = [[] for _ in range(N_GROUPS)]
    nc_p, nc_s = [], []
    for l in range(depth):
        lw = dict(g_att=norm_att_out[l], g_conv=norm_conv_out[l], w_out=w_out[l].astype(bf),
                  g_ffn=norm_ffn[l], w_pq=w_pq[l].astype(bf), ka=sub_keys_a[l].astype(bf),
                  kb=sub_keys_b[l].astype(bf), u=expert_u[l].astype(bf), v=expert_v[l].astype(bf))
        w_in_bf = w_in[l].astype(bf)

        q, k, v, pc = _project_in(hp, norm_mix[l], w_in_bf)
        att = _prompt_attention(q, k, v, b, s)
        cy, ns = _gated_conv(pc, jnp.zeros((b, CONV_K - 1, conv_c), jnp.float32), conv_w[l], b, s, 256)
        k5 = k.reshape(b, s, N_GROUPS, ATT_SLOTS, HEAD_DIM)
        v5 = v.reshape(b, s, N_GROUPS, ATT_SLOTS, HEAD_DIM)
        for g, (win, _) in enumerate(DIL_PATTERNS):
            rows = min(win, s)
            nk_p[g].append(k5[:, s - rows:, g])
            nv_p[g].append(v5[:, s - rows:, g])
        nc_p.append(ns)
        h_res, peer = _channel_mix(hp, att, cy, lw)
        last = l == depth - 1
        if last:
            y_prompt = _final(h_res, peer, norm_final, _tile(b * s, 256))
        else:
            hp = h_res + peer

        q, k, v, pc = _project_in(hs, norm_mix[l], w_in_bf)
        ck = [c[l].reshape(bd, c.shape[2], ATT_WIDTH) for c in caches_k]
        cv = [c[l].reshape(bd, c.shape[2], ATT_WIDTH) for c in caches_v]
        att, new_caches = _sample_attention(q, k, v, ck, cv, bd, td)
        for g in range(N_GROUPS):
            shape5 = (bd, ck[g].shape[1], ATT_SLOTS, HEAD_DIM)
            nk_s[g].append(new_caches[2 * g].reshape(shape5))
            nv_s[g].append(new_caches[2 * g + 1].reshape(shape5))
        cy, ns = _gated_conv(pc, state_conv[l], conv_w[l], bd, td, conv_c)
        nc_s.append(ns)
        h_res, peer = _channel_mix(hs, att, cy, lw)
        if last:
            y_sample = _final(h_res, peer, norm_final, _tile(bd * td, 256))
        else:
            hs = h_res + peer

    return (y_prompt.reshape(b, s, d), y_sample.reshape(bd, td, d),
            jnp.stack(nk_p[0], 0), jnp.stack(nv_p[0], 0),
            jnp.stack(nk_p[1], 0), jnp.stack(nv_p[1], 0),
            jnp.stack(nk_p[2], 0), jnp.stack(nv_p[2], 0),
            jnp.stack(nc_p, 0),
            jnp.stack(nk_s[0], 0), jnp.stack(nv_s[0], 0),
            jnp.stack(nk_s[1], 0), jnp.stack(nv_s[1], 0),
            jnp.stack(nk_s[2], 0), jnp.stack(nv_s[2], 0),
            jnp.stack(nc_s, 0))
```

```python
import functools
import math

import jax
import jax.numpy as jnp
from jax import lax
from jax.experimental import pallas as pl
from jax.experimental.pallas import tpu as pltpu

HEAD_DIM = 64
ATT_SLOTS = 16
DIL_PATTERNS = ((128, 1), (512, 4), (2048, 16))
N_GROUPS = len(DIL_PATTERNS)
ATT_WIDTH = ATT_SLOTS * HEAD_DIM
QKV_WIDTH = N_GROUPS * ATT_WIDTH
CONV_K = 3
N_KEYS = 128
PEER_HEADS = 8
PEER_QDIM = 256
PEER_HALF = PEER_QDIM // 2
PEER_TOPK = 16
NORM_EPS = 1e-6
NEG_INF = -1e30

LANES = 128
HEADS_PER_SLAB = LANES // HEAD_DIM
N_SLABS = ATT_WIDTH // LANES
N_BACK = 128
VMEM_LIMIT = 56 * 1024 * 1024

_NT = (((1,), (1,)), ((), ()))
_TN = (((0,), (0,)), ((), ()))


def _params(vmem=None):
    return pltpu.CompilerParams(vmem_limit_bytes=vmem or VMEM_LIMIT)


def _rms(x, g):
    return x * lax.rsqrt(jnp.mean(x * x, axis=-1, keepdims=True) + NORM_EPS) * g


def _rmsnorm_kernel(x_ref, g_ref, o_ref):
    o_ref[...] = _rms(x_ref[...], g_ref[...]).astype(o_ref.dtype)


def _rmsnorm(x, g, out_dtype, tm):
    t, d = x.shape
    return pl.pallas_call(
        _rmsnorm_kernel,
        grid=(t // tm,),
        in_specs=[pl.BlockSpec((tm, d), lambda i: (i, 0)),
                  pl.BlockSpec((1, d), lambda i: (0, 0))],
        out_specs=pl.BlockSpec((tm, d), lambda i: (i, 0)),
        out_shape=jax.ShapeDtypeStruct((t, d), out_dtype),
        compiler_params=_params(),
        name="rmsnorm",
    )(x, g.reshape(1, d))


def _mm_kernel(a_ref, b_ref, o_ref):
    o_ref[...] = jnp.dot(a_ref[...], b_ref[...],
                         preferred_element_type=jnp.float32).astype(o_ref.dtype)


def _matmul(a, b, col_start, n_cols, out_dtype, tm, tn):
    t, k = a.shape
    off = col_start // tn
    return pl.pallas_call(
        _mm_kernel,
        grid=(t // tm, n_cols // tn),
        in_specs=[pl.BlockSpec((tm, k), lambda i, j: (i, 0)),
                  pl.BlockSpec((k, tn), lambda i, j: (0, j + off))],
        out_specs=pl.BlockSpec((tm, tn), lambda i, j: (i, j)),
        out_shape=jax.ShapeDtypeStruct((t, n_cols), out_dtype),
        compiler_params=_params(),
        name="matmul",
    )(a, b)


def _conv_kernel(gb_ref, gc_ref, h_ref, st_ref, w_ref, y_ref, ns_ref):
    u = gc_ref[...] * h_ref[...]
    s = u.shape[0]
    row = lax.broadcasted_iota(jnp.int32, u.shape, 0)
    st = st_ref[...]
    w = w_ref[...]
    u1 = jnp.where(row == 0, st[1:2, :], pltpu.roll(u, 1, 0))
    u2 = jnp.where(row == 0, st[0:1, :],
                   jnp.where(row == 1, st[1:2, :], pltpu.roll(u, 2, 0)))
    y = u2 * w[0:1, :]
    y = y + u1 * w[1:2, :]
    y = y + u * w[2:3, :]
    y_ref[...] = gb_ref[...] * y
    ns_ref[...] = u[s - 2:s, :]


def _gated_conv(pc, state, conv_w, n_seq, seq, cb):
    c = conv_w.shape[1]
    ncb = c // cb
    return pl.pallas_call(
        _conv_kernel,
        grid=(n_seq, ncb),
        in_specs=[pl.BlockSpec((seq, cb), lambda b, j: (b, j)),
                  pl.BlockSpec((seq, cb), lambda b, j: (b, ncb + j)),
                  pl.BlockSpec((seq, cb), lambda b, j: (b, 2 * ncb + j)),
                  pl.BlockSpec((None, CONV_K - 1, cb), lambda b, j: (b, 0, j)),
                  pl.BlockSpec((CONV_K, cb), lambda b, j: (0, j))],
        out_specs=[pl.BlockSpec((seq, cb), lambda b, j: (b, j)),
                   pl.BlockSpec((None, CONV_K - 1, cb), lambda b, j: (b, 0, j))],
        out_shape=[jax.ShapeDtypeStruct((n_seq * seq, c), jnp.float32),
                   jax.ShapeDtypeStruct((n_seq, CONV_K - 1, c), jnp.float32)],
        compiler_params=_params(),
        name="gated_conv",
    )(pc, pc, pc, state, conv_w)


def _slope_like(shape, group, head):
    i = (group * ATT_SLOTS + 1 + head).astype(jnp.float32)
    n = float(N_GROUPS * ATT_SLOTS)
    return jnp.exp2(jnp.broadcast_to(-8.0 * i / n, shape))


def _prompt_attn_kernel(q0, k0, v0, q1, k1, v1, q2, k2, v2, o_ref, *stat_refs, seq):
    slab = pl.program_id(1)
    lane = lax.broadcasted_iota(jnp.int32, (1, LANES), 1)
    first = lane < HEAD_DIM
    scale = HEAD_DIM ** -0.5

    refs = ((q0, k0, v0), (q1, k1, v1), (q2, k2, v2))
    for g, (win, dil) in enumerate(DIL_PATTERNS):
        q_ref, k_ref, v_ref = refs[g]
        m_ref, l_ref, n_ref = stat_refs[3 * g:3 * g + 3]
        span = dil * N_BACK
        nb = seq // span
        two = nb > 1
        nk = (2 if two else 1) * N_BACK
        qi = lax.broadcasted_iota(jnp.int32, (N_BACK, nk), 0)
        kc = lax.broadcasted_iota(jnp.int32, (N_BACK, nk), 1)
        steps = (N_BACK if two else 0) + qi - kc
        in_band = (steps >= 0) & (steps <= N_BACK)
        dist = (steps * dil).astype(jnp.float32)
        slopes = [_slope_like((1, nk), g, slab * HEADS_PER_SLAB + h)
                  for h in range(HEADS_PER_SLAB)]

        def tile(t, carry, q_ref=q_ref, k_ref=k_ref, v_ref=v_ref, m_ref=m_ref, l_ref=l_ref,
                 n_ref=n_ref, dil=dil, span=span, nb=nb, two=two, kc=kc, in_band=in_band,
                 dist=dist, slopes=slopes):
            r = t // nb
            n = t - r * nb
            start = n * span + r
            rows = pl.ds(start, N_BACK, stride=dil)
            q = q_ref[rows, :] * scale
            if two:
                prev = jnp.maximum(start - span, r)
                prows = pl.ds(prev, N_BACK, stride=dil)
                kk = jnp.concatenate([k_ref[prows, :], k_ref[rows, :]], axis=0)
                vv = jnp.concatenate([v_ref[prows, :], v_ref[rows, :]], axis=0)
                valid = in_band & ((kc >= N_BACK) | (n > 0))
            else:
                kk, vv, valid = k_ref[rows, :], v_ref[rows, :], in_band
            kk = kk.astype(jnp.bfloat16)
            vv = vv.astype(jnp.bfloat16)
            stats = []
            for h in range(HEADS_PER_SLAB):
                mine = first if h == 0 else jnp.logical_not(first)
                qm = jnp.where(mine, q, 0.0).astype(jnp.bfloat16)
                s = lax.dot_general(qm, kk, _NT, preferred_element_type=jnp.float32)
                s = s + jnp.where(valid, -slopes[h] * dist, NEG_INF)
                m = jnp.max(s, axis=-1, keepdims=True)
                p = jnp.exp(s - m)
                l = jnp.sum(p, axis=-1, keepdims=True)
                o = jnp.dot(p.astype(jnp.bfloat16), vv, preferred_element_type=jnp.float32)
                stats.append((m, l, o))
            m_ref[rows, :] = jnp.where(first, stats[0][0], stats[1][0])
            l_ref[rows, :] = jnp.where(first, stats[0][1], stats[1][1])
            n_ref[rows, :] = jnp.where(first, stats[0][2], stats[1][2])
            return carry

        lax.fori_loop(0, dil * nb, tile, 0, unroll=4)

    chunk = 256

    def merge(c, carry):
        rows = pl.ds(pl.multiple_of(c * chunk, chunk), chunk)
        ms = [stat_refs[3 * g][rows, :] for g in range(N_GROUPS)]
        m = functools.reduce(jnp.maximum, ms)
        den = num = None
        for g in range(N_GROUPS):
            a = jnp.exp(ms[g] - m)
            dg = a * stat_refs[3 * g + 1][rows, :]
            ng = a * stat_refs[3 * g + 2][rows, :]
            den = dg if den is None else den + dg
            num = ng if num is None else num + ng
        o_ref[rows, :] = num / den
        return carry

    lax.fori_loop(0, seq // chunk, merge, 0)


def _prompt_attention(q, k, v, n_seq, seq):
    assert seq % (DIL_PATTERNS[-1][1] * N_BACK) == 0
    in_specs = []
    for g in range(N_GROUPS):
        for _ in range(3):
            in_specs.append(pl.BlockSpec((seq, LANES), lambda b, j, g=g: (b, g * N_SLABS + j)))
    args = []
    for g in range(N_GROUPS):
        args += [q, k, v]
    return pl.pallas_call(
        functools.partial(_prompt_attn_kernel, seq=seq),
        grid=(n_seq, N_SLABS),
        in_specs=in_specs,
        out_specs=pl.BlockSpec((seq, LANES), lambda b, j: (b, j)),
        out_shape=jax.ShapeDtypeStruct((n_seq * seq, ATT_WIDTH), jnp.float32),
        scratch_shapes=[pltpu.VMEM((seq, LANES), jnp.float32)] * (3 * N_GROUPS),
        compiler_params=_params(),
        name="prompt_attention",
    )(*args)


def _sample_attn_kernel(*refs, t_new):
    ins = refs[:5 * N_GROUPS]
    o_ref = refs[5 * N_GROUPS]
    scr = refs[5 * N_GROUPS + 1:]
    slab = pl.program_id(1)
    lane = lax.broadcasted_iota(jnp.int32, (1, LANES), 1)
    head = slab * HEADS_PER_SLAB + (lane >= HEAD_DIM).astype(jnp.int32)
    scale = HEAD_DIM ** -0.5
    jr = lax.broadcasted_iota(jnp.int32, (LANES, LANES), 0) < HEAD_DIM
    jc = lax.broadcasted_iota(jnp.int32, (LANES, LANES), 1) < HEAD_DIM
    same_head = jnp.where(jr == jc, 1.0, 0.0).astype(jnp.bfloat16)
    back = (N_BACK - lax.broadcasted_iota(jnp.int32, (N_BACK, 1), 0)).astype(jnp.float32)

    m_run = l_run = o_run = None
    for g, (win, dil) in enumerate(DIL_PATTERNS):
        qn_ref, kn_ref, vn_ref, ck_ref, cv_ref = ins[5 * g:5 * g + 5]
        kall, vall = scr[2 * g:2 * g + 2]
        w = ck_ref.shape[0]
        kn = kn_ref[...]
        vn = vn_ref[...]
        kall[0:w, :] = ck_ref[...]
        kall[w:w + t_new, :] = kn
        vall[0:w, :] = cv_ref[...]
        vall[w:w + t_new, :] = vn

        qs = qn_ref[...] * scale
        bias = -_slope_like((1, LANES), g, head) * (back * float(dil))
        prods = [kall[pl.ds(t, N_BACK, stride=dil), :] * qs[t:t + 1, :] for t in range(t_new)]
        prods.append(kn * qs)
        prod = jnp.concatenate(prods, axis=0)
        hi = prod.astype(jnp.bfloat16)
        lo = (prod - hi.astype(jnp.float32)).astype(jnp.bfloat16)
        sc = (jnp.dot(hi, same_head, preferred_element_type=jnp.float32)
              + jnp.dot(lo, same_head, preferred_element_type=jnp.float32))
        s_self = sc[t_new * N_BACK:, :]
        ms, ls, os_ = [], [], []
        for t in range(t_new):
            s = sc[t * N_BACK:(t + 1) * N_BACK, :] + bias
            ss = s_self[t:t + 1, :]
            m = jnp.maximum(jnp.max(s, axis=0, keepdims=True), ss)
            p = jnp.exp(s - m)
            p_self = jnp.exp(ss - m)
            vt = vall[pl.ds(t, N_BACK, stride=dil), :]
            ms.append(m)
            ls.append(jnp.sum(p, axis=0, keepdims=True) + p_self)
            os_.append(jnp.sum(p * vt, axis=0, keepdims=True) + p_self * vn[t:t + 1, :])
        m_g = jnp.concatenate(ms, axis=0)
        l_g = jnp.concatenate(ls, axis=0)
        o_g = jnp.concatenate(os_, axis=0)
        if m_run is None:
            m_run, l_run, o_run = m_g, l_g, o_g
        else:
            m_new = jnp.maximum(m_run, m_g)
            a_old = jnp.exp(m_run - m_new)
            a_g = jnp.exp(m_g - m_new)
            l_run = a_old * l_run + a_g * l_g
            o_run = a_old * o_run + a_g * o_g
            m_run = m_new
    o_ref[...] = o_run / l_run


def _sample_attention(q, k, v, caches_k, caches_v, n_seq, t_new):
    in_specs, args, out_specs, out_shape, scratch = [], [], [], [], []
    for g in range(N_GROUPS):
        w = caches_k[g].shape[1]
        assert w == DIL_PATTERNS[g][0], "cache holds exactly one window"
        new_spec = pl.BlockSpec((t_new, LANES), lambda b, j, g=g: (b, g * N_SLABS + j))
        cache_spec = pl.BlockSpec((None, w, LANES), lambda b, j: (b, 0, j))
        in_specs += [new_spec, new_spec, new_spec, cache_spec, cache_spec]
        args += [q, k, v, caches_k[g], caches_v[g]]
        scratch += [pltpu.VMEM((w + t_new, LANES), jnp.float32)] * 2
    out_specs.append(pl.BlockSpec((t_new, LANES), lambda b, j: (b, j)))
    out_shape.append(jax.ShapeDtypeStruct((n_seq * t_new, ATT_WIDTH), jnp.float32))
    res = pl.pallas_call(
        functools.partial(_sample_attn_kernel, t_new=t_new),
        grid=(n_seq, N_SLABS),
        in_specs=in_specs,
        out_specs=out_specs,
        out_shape=out_shape,
        scratch_shapes=scratch,
        compiler_params=_params(),
        name="sample_attention",
    )(*args)
    return res[0]


def _catnorm_kernel(att_ref, cy_ref, ga_ref, gc_ref, o_ref):
    wa = att_ref.shape[1]
    o_ref[:, 0:wa] = _rms(att_ref[...], ga_ref[...]).astype(o_ref.dtype)
    o_ref[:, wa:] = _rms(cy_ref[...], gc_ref[...]).astype(o_ref.dtype)


def _catnorm(att, cy, g_att, g_conv, tm):
    t, wa = att.shape
    wc = cy.shape[1]
    return pl.pallas_call(
        _catnorm_kernel,
        grid=(t // tm,),
        in_specs=[pl.BlockSpec((tm, wa), lambda i: (i, 0)),
                  pl.BlockSpec((tm, wc), lambda i: (i, 0)),
                  pl.BlockSpec((1, wa), lambda i: (0, 0)),
                  pl.BlockSpec((1, wc), lambda i: (0, 0))],
        out_specs=pl.BlockSpec((tm, wa + wc), lambda i: (i, 0)),
        out_shape=jax.ShapeDtypeStruct((t, wa + wc), jnp.bfloat16),
        compiler_params=_params(),
        name="catnorm",
    )(att, cy, g_att.reshape(1, wa), g_conv.reshape(1, wc))


def _outproj_kernel(a_ref, w_ref, x_ref, g_ref, h_ref, hn_ref):
    h = x_ref[...] + jnp.dot(a_ref[...], w_ref[...], preferred_element_type=jnp.float32)
    h_ref[...] = h
    hn_ref[...] = _rms(h, g_ref[...]).astype(hn_ref.dtype)


def _outproj(cat, w_out, x, g_ffn, tm):
    t, k = cat.shape
    d = w_out.shape[1]
    return pl.pallas_call(
        _outproj_kernel,
        grid=(t // tm,),
        in_specs=[pl.BlockSpec((tm, k), lambda i: (i, 0)),
                  pl.BlockSpec((k, d), lambda i: (0, 0)),
                  pl.BlockSpec((tm, d), lambda i: (i, 0)),
                  pl.BlockSpec((1, d), lambda i: (0, 0))],
        out_specs=[pl.BlockSpec((tm, d), lambda i: (i, 0)),
                   pl.BlockSpec((tm, d), lambda i: (i, 0))],
        out_shape=[jax.ShapeDtypeStruct((t, d), jnp.float32),
                   jax.ShapeDtypeStruct((t, d), jnp.bfloat16)],
        compiler_params=_params(),
        name="outproj",
    )(cat, w_out, x, g_ffn.reshape(1, d))


_N_RANK = PEER_TOPK + 1
_CAND_PER_RANK = tuple(_N_RANK // (i + 1) for i in range(_N_RANK))
_N_CAND = sum(_CAND_PER_RANK)
_CAND_ROWS = -(-_N_CAND // 8) * 8


def _top_values(xs, k):
    tt = xs[0].shape[1]
    slot = lax.broadcasted_iota(jnp.int32, (k, tt), 0)
    rowfs = [lax.broadcasted_iota(jnp.int32, x.shape, 0).astype(jnp.float32) for x in xs]

    def step(r, carry):
        new = []
        for (x, vals), rowf in zip(carry, rowfs):
            m = jnp.max(x, axis=0, keepdims=True)
            first = jnp.min(jnp.where(x == m, rowf, float(x.shape[0])), axis=0, keepdims=True)
            new.append((jnp.where(rowf == first, -jnp.inf, x), jnp.where(slot == r, m, vals)))
        return tuple(new)

    init = tuple((x, jnp.zeros((k, tt), jnp.float32)) for x in xs)
    return [vals for _, vals in lax.fori_loop(0, k, step, init)]


def _route_kernel(q_ref, ka_ref, kb_ref, thr_ref, sb_ref, ea_ref, eb_ref):
    ka = ka_ref[...]
    kb = kb_ref[...]
    tt = q_ref.shape[0]
    sub = N_KEYS // 8

    def per_head(h, carry):
        c0 = pl.multiple_of(h * PEER_QDIM, PEER_QDIM)
        qa = q_ref[:, pl.ds(c0, PEER_HALF)]
        qb = q_ref[:, pl.ds(c0 + PEER_HALF, PEER_HALF)]
        sa = lax.dot_general(ka, qa, _NT, preferred_element_type=jnp.float32)
        sb = lax.dot_general(kb, qb, _NT, preferred_element_type=jnp.float32)
        va, vb = _top_values((sa, sb), _N_RANK)
        pieces = [va[i:i + 1, :] + vb[0:n, :] for i, n in enumerate(_CAND_PER_RANK)]
        pieces.append(jnp.full((_CAND_ROWS - _N_CAND, tt), -jnp.inf, jnp.float32))
        (top,) = _top_values((jnp.concatenate(pieces, axis=0),), _N_RANK)
        z = jnp.sum(jnp.exp(top[0:PEER_TOPK, :] - top[0:1, :]), axis=0, keepdims=True)
        tau = 0.5 * (top[PEER_TOPK - 1:PEER_TOPK, :] + top[PEER_TOPK:PEER_TOPK + 1, :])
        thr = tau - sa
        ea = jnp.exp(sa - va[0:1, :])
        eb = jnp.exp(sb - vb[0:1, :]) / z
        for lt in range(tt // LANES):
            ls = slice(lt * LANES, (lt + 1) * LANES)
            thr_ref[h, lt] = thr[:, ls].reshape(sub, 8, LANES)
            ea_ref[h, lt] = ea[:, ls].reshape(sub, 8, LANES)
            sb_ref[h, lt] = sb[:, ls]
            eb_ref[h, lt] = eb[:, ls]
        return carry

    lax.fori_loop(0, PEER_HEADS, per_head, 0)


def _route_specs(tt, imap):
    n_lt = tt // LANES
    row = pl.BlockSpec((PEER_HEADS, n_lt, N_KEYS // 8, 8, LANES), lambda *g: (0, imap(*g), 0, 0, 0))
    full = pl.BlockSpec((PEER_HEADS, n_lt, N_KEYS, LANES), lambda *g: (0, imap(*g), 0, 0))
    return [row, full, row, full]


def _route(qp, ka, kb, tt):
    t = qp.shape[0]
    row_shape = jax.ShapeDtypeStruct((PEER_HEADS, t // LANES, N_KEYS // 8, 8, LANES), jnp.float32)
    full_shape = jax.ShapeDtypeStruct((PEER_HEADS, t // LANES, N_KEYS, LANES), jnp.float32)
    return pl.pallas_call(
        _route_kernel,
        grid=(t // tt,),
        in_specs=[pl.BlockSpec((tt, PEER_HEADS * PEER_QDIM), lambda i: (i, 0)),
                  pl.BlockSpec((N_KEYS, PEER_HALF), lambda i: (0, 0)),
                  pl.BlockSpec((N_KEYS, PEER_HALF), lambda i: (0, 0))],
        out_specs=_route_specs(tt, lambda i: i),
        out_shape=[row_shape, full_shape, row_shape, full_shape],
        compiler_params=_params(),
        name="peer_route",
    )(qp, ka, kb)


def _gelu(x):
    return 0.5 * x * (1.0 + lax.erf(x * math.sqrt(0.5)))


def _roll_copies(layer, cache_refs, new_refs, out_refs, sems):
    copies = []
    for j, (c, n, o) in enumerate(zip(cache_refs, new_refs, out_refs)):
        w, t = c.shape[2], n.shape[1]
        copies.append(pltpu.make_async_copy(c.at[layer, :, pl.ds(t, w - t)],
                                            o.at[:, pl.ds(0, w - t)], sems.at[2 * j]))
        copies.append(pltpu.make_async_copy(n, o.at[:, pl.ds(w - t, t)], sems.at[2 * j + 1]))
    return copies


def _peer_kernel(hn_ref, u_ref, v_ref, thr_ref, sb_ref, ea_ref, eb_ref, *rest, n_roll, layer):
    cache_refs, new_refs = rest[:n_roll], rest[n_roll:2 * n_roll]
    o_ref = rest[2 * n_roll]
    out_refs = rest[2 * n_roll + 1:3 * n_roll + 1]
    h_scr, w_scr = rest[3 * n_roll + 1:3 * n_roll + 3]
    i = pl.program_id(0)
    e = pl.program_id(1)
    et, tt = h_scr.shape
    assert et == 8 * N_KEYS

    if n_roll:
        copies = _roll_copies(layer, cache_refs, new_refs, out_refs, rest[3 * n_roll + 3])

        @pl.when((i == 0) & (e == 0))
        def _():
            for c in copies:
                c.start()

    @pl.when(e == 0)
    def _():
        o_ref[...] = jnp.zeros(o_ref.shape, o_ref.dtype)

    h_scr[...] = lax.dot_general(u_ref[...], hn_ref[...], _NT,
                                 preferred_element_type=jnp.float32)

    def per_lane_tile(lt, carry):
        ls = pl.ds(pl.multiple_of(lt * LANES, LANES), LANES)
        for a in range(et // N_KEYS):
            rows = pl.ds(a * N_KEYS, N_KEYS)
            acc = None
            for h in range(PEER_HEADS):
                keep = sb_ref[h, lt] >= thr_ref[h, lt, e, a:a + 1, :]
                gate = jnp.where(keep, eb_ref[h, lt], 0.0) * ea_ref[h, lt, e, a:a + 1, :]
                acc = gate if acc is None else acc + gate
            w_scr[rows, ls] = (acc * _gelu(h_scr[rows, ls])).astype(w_scr.dtype)
        return carry

    lax.fori_loop(0, tt // LANES, per_lane_tile, 0)
    o_ref[...] += lax.dot_general(w_scr[...], v_ref[...], _TN,
                                  preferred_element_type=jnp.float32)

    if n_roll:
        @pl.when((i == pl.num_programs(0) - 1) & (e == pl.num_programs(1) - 1))
        def _():
            for c in copies:
                c.wait()


def _peer_dense(hn, u, v, routing, tt, et, roll=None):
    t, d = hn.shape
    n_exp = u.shape[0]
    layer, caches, new_rows = roll if roll else (0, [], [])
    n_roll = len(caches)
    any_spec = pl.BlockSpec(memory_space=pl.ANY)
    res = pl.pallas_call(
        functools.partial(_peer_kernel, n_roll=n_roll, layer=layer),
        grid=(t // tt, n_exp // et),
        in_specs=[pl.BlockSpec((tt, d), lambda i, e: (i, 0)),
                  pl.BlockSpec((et, d), lambda i, e: (e, 0)),
                  pl.BlockSpec((et, d), lambda i, e: (e, 0))] + _route_specs(tt, lambda i, e: i)
        + [any_spec] * (2 * n_roll),
        out_specs=[pl.BlockSpec((tt, d), lambda i, e: (i, 0))] + [any_spec] * n_roll,
        out_shape=[jax.ShapeDtypeStruct((t, d), jnp.float32)]
        + [jax.ShapeDtypeStruct(c.shape[1:], c.dtype) for c in caches],
        scratch_shapes=[pltpu.VMEM((et, tt), jnp.float32),
                        pltpu.VMEM((et, tt), jnp.bfloat16)]
        + ([pltpu.SemaphoreType.DMA((2 * n_roll,))] if n_roll else []),
        compiler_params=_params(),
        name="peer_dense",
    )(hn, u, v, *routing, *caches, *new_rows)
    return res[0], res[1:]


def _final_kernel(h_ref, p_ref, g_ref, o_ref):
    o_ref[...] = _rms(h_ref[...] + p_ref[...], g_ref[...])


def _final(h, p, g, tm):
    t, d = h.shape
    row = pl.BlockSpec((tm, d), lambda i: (i, 0))
    return pl.pallas_call(
        _final_kernel,
        grid=(t // tm,),
        in_specs=[row, row, pl.BlockSpec((1, d), lambda i: (0, 0))],
        out_specs=row,
        out_shape=jax.ShapeDtypeStruct((t, d), jnp.float32),
        compiler_params=_params(),
        name="final_norm",
    )(h, p, g.reshape(1, d))


def _tile(t, pref):
    while t % pref:
        pref //= 2
    return pref


def _project_in(x2, norm_g, w_in_bf):
    t = x2.shape[0]
    xn = _rmsnorm(x2, norm_g, jnp.bfloat16, _tile(t, 256))
    tm = _tile(t, 1024)
    cw = (w_in_bf.shape[1] - 3 * QKV_WIDTH)
    q = _matmul(xn, w_in_bf, 0, QKV_WIDTH, jnp.float32, tm, 1024)
    k = _matmul(xn, w_in_bf, QKV_WIDTH, QKV_WIDTH, jnp.float32, tm, 1024)
    v = _matmul(xn, w_in_bf, 2 * QKV_WIDTH, QKV_WIDTH, jnp.float32, tm, 1024)
    pc = _matmul(xn, w_in_bf, 3 * QKV_WIDTH, cw, jnp.float32, tm, 1024)
    return q, k, v, pc


def _channel_mix(x2, att, cy, lw, roll=None):
    t = x2.shape[0]
    cat = _catnorm(att, cy, lw["g_att"], lw["g_conv"], _tile(t, 256))
    h, hn = _outproj(cat, lw["w_out"], x2, lw["g_ffn"], _tile(t, 256))
    qp = _matmul(hn, lw["w_pq"], 0, lw["w_pq"].shape[1], jnp.bfloat16, _tile(t, 1024), 1024)
    routing = _route(qp, lw["ka"], lw["kb"], _tile(t, 256))
    peer, rolled = _peer_dense(hn, lw["u"], lw["v"], routing, _tile(t, 512), 8 * N_KEYS, roll)
    return h, peer, rolled


def kernel(x_prompt, x_sample, cache_k_g0, cache_v_g0, cache_k_g1, cache_v_g1, cache_k_g2, cache_v_g2, state_conv, norm_mix, w_in, conv_w, norm_att_out, norm_conv_out, w_out, norm_ffn, w_pq, sub_keys_a, sub_keys_b, expert_u, expert_v, norm_final):
    bf = jnp.bfloat16
    depth = w_in.shape[0]
    b, s, d = x_prompt.shape
    bd, td, _ = x_sample.shape
    caches_k = (cache_k_g0, cache_k_g1, cache_k_g2)
    caches_v = (cache_v_g0, cache_v_g1, cache_v_g2)
    conv_c = conv_w.shape[2]

    hp = x_prompt.reshape(b * s, d)
    hs = x_sample.reshape(bd * td, d)
    nk_p = [[] for _ in range(N_GROUPS)]
    nv_p = [[] for _ in range(N_GROUPS)]
    nk_s = [[] for _ in range(N_GROUPS)]
    nv_s = [[] for _ in range(N_GROUPS)]
    nc_p, nc_s = [], []
    for l in range(depth):
        lw = dict(g_att=norm_att_out[l], g_conv=norm_conv_out[l], w_out=w_out[l].astype(bf),
                  g_ffn=norm_ffn[l], w_pq=w_pq[l].astype(bf), ka=sub_keys_a[l].astype(bf),
                  kb=sub_keys_b[l].astype(bf), u=expert_u[l].astype(bf), v=expert_v[l].astype(bf))
        w_in_bf = w_in[l].astype(bf)

        qs, ks, vs, pcs = _project_in(hs, norm_mix[l], w_in_bf)
        ck = [c[l].reshape(bd, c.shape[2], ATT_WIDTH) for c in caches_k]
        cv = [c[l].reshape(bd, c.shape[2], ATT_WIDTH) for c in caches_v]
        att_s = _sample_attention(qs, ks, vs, ck, cv, bd, td)
        cy_s, ns_s = _gated_conv(pcs, state_conv[l], conv_w[l], bd, td, conv_c)
        nc_s.append(ns_s)
        ks5 = ks.reshape(bd, td, N_GROUPS, ATT_SLOTS, HEAD_DIM)
        vs5 = vs.reshape(bd, td, N_GROUPS, ATT_SLOTS, HEAD_DIM)
        roll_caches, roll_new = [], []
        for g in range(N_GROUPS):
            roll_caches += [caches_k[g], caches_v[g]]
            roll_new += [ks5[:, :, g], vs5[:, :, g]]

        q, k, v, pc = _project_in(hp, norm_mix[l], w_in_bf)
        att = _prompt_attention(q, k, v, b, s)
        cy, ns = _gated_conv(pc, jnp.zeros((b, CONV_K - 1, conv_c), jnp.float32), conv_w[l], b, s, 256)
        k5 = k.reshape(b, s, N_GROUPS, ATT_SLOTS, HEAD_DIM)
        v5 = v.reshape(b, s, N_GROUPS, ATT_SLOTS, HEAD_DIM)
        for g, (win, _) in enumerate(DIL_PATTERNS):
            rows = min(win, s)
            nk_p[g].append(k5[:, s - rows:, g])
            nv_p[g].append(v5[:, s - rows:, g])
        nc_p.append(ns)
        h_res, peer, rolled = _channel_mix(hp, att, cy, lw, (l, roll_caches, roll_new))
        for g in range(N_GROUPS):
            nk_s[g].append(rolled[2 * g])
            nv_s[g].append(rolled[2 * g + 1])
        last = l == depth - 1
        if last:
            y_prompt = _final(h_res, peer, norm_final, _tile(b * s, 256))
        else:
            hp = h_res + peer

        h_res, peer, _ = _channel_mix(hs, att_s, cy_s, lw)
        if last:
            y_sample = _final(h_res, peer, norm_final, _tile(bd * td, 256))
        else:
            hs = h_res + peer

    return (y_prompt.reshape(b, s, d), y_sample.reshape(bd, td, d),
            jnp.stack(nk_p[0], 0), jnp.stack(nv_p[0], 0),
            jnp.stack(nk_p[1], 0), jnp.stack(nv_p[1], 0),
            jnp.stack(nk_p[2], 0), jnp.stack(nv_p[2], 0),
            jnp.stack(nc_p, 0),
            jnp.stack(nk_s[0], 0), jnp.stack(nv_s[0], 0),
            jnp.stack(nk_s[1], 0), jnp.stack(nv_s[1], 0),
            jnp.stack(nk_s[2], 0), jnp.stack(nv_s[2], 0),
            jnp.stack(nc_s, 0))
```

```python
import functools
import math

import jax
import jax.numpy as jnp
from jax import lax
from jax.experimental import pallas as pl
from jax.experimental.pallas import tpu as pltpu

HEAD_DIM = 64
ATT_SLOTS = 16
DIL_PATTERNS = ((128, 1), (512, 4), (2048, 16))
N_GROUPS = len(DIL_PATTERNS)
ATT_WIDTH = ATT_SLOTS * HEAD_DIM
QKV_WIDTH = N_GROUPS * ATT_WIDTH
CONV_K = 3
N_KEYS = 128
PEER_HEADS = 8
PEER_QDIM = 256
PEER_HALF = PEER_QDIM // 2
PEER_TOPK = 16
NORM_EPS = 1e-6
NEG_INF = -1e30

LANES = 128
HEADS_PER_SLAB = LANES // HEAD_DIM
N_SLABS = ATT_WIDTH // LANES
N_BACK = 128
VMEM_LIMIT = 56 * 1024 * 1024

_NT = (((1,), (1,)), ((), ()))
_TN = (((0,), (0,)), ((), ()))


def _params(vmem=None):
    return pltpu.CompilerParams(vmem_limit_bytes=vmem or VMEM_LIMIT)


def _rms(x, g):
    return x * lax.rsqrt(jnp.mean(x * x, axis=-1, keepdims=True) + NORM_EPS) * g


def _rmsnorm_kernel(x_ref, g_ref, o_ref):
    o_ref[...] = _rms(x_ref[...], g_ref[...]).astype(o_ref.dtype)


def _rmsnorm(x, g, out_dtype, tm):
    t, d = x.shape
    return pl.pallas_call(
        _rmsnorm_kernel,
        grid=(t // tm,),
        in_specs=[pl.BlockSpec((tm, d), lambda i: (i, 0)),
                  pl.BlockSpec((1, d), lambda i: (0, 0))],
        out_specs=pl.BlockSpec((tm, d), lambda i: (i, 0)),
        out_shape=jax.ShapeDtypeStruct((t, d), out_dtype),
        compiler_params=_params(),
        name="rmsnorm",
    )(x, g.reshape(1, d))


def _mm_kernel(a_ref, b_ref, o_ref):
    o_ref[...] = jnp.dot(a_ref[...], b_ref[...],
                         preferred_element_type=jnp.float32).astype(o_ref.dtype)


def _matmul(a, b, col_start, n_cols, out_dtype, tm, tn):
    t, k = a.shape
    off = col_start // tn
    return pl.pallas_call(
        _mm_kernel,
        grid=(t // tm, n_cols // tn),
        in_specs=[pl.BlockSpec((tm, k), lambda i, j: (i, 0)),
                  pl.BlockSpec((k, tn), lambda i, j: (0, j + off))],
        out_specs=pl.BlockSpec((tm, tn), lambda i, j: (i, j)),
        out_shape=jax.ShapeDtypeStruct((t, n_cols), out_dtype),
        compiler_params=_params(),
        name="matmul",
    )(a, b)


def _conv_kernel(gb_ref, gc_ref, h_ref, st_ref, w_ref, y_ref, ns_ref):
    u = gc_ref[...] * h_ref[...]
    s = u.shape[0]
    row = lax.broadcasted_iota(jnp.int32, u.shape, 0)
    st = st_ref[...]
    w = w_ref[...]
    u1 = jnp.where(row == 0, st[1:2, :], pltpu.roll(u, 1, 0))
    u2 = jnp.where(row == 0, st[0:1, :],
                   jnp.where(row == 1, st[1:2, :], pltpu.roll(u, 2, 0)))
    y = u2 * w[0:1, :]
    y = y + u1 * w[1:2, :]
    y = y + u * w[2:3, :]
    y_ref[...] = gb_ref[...] * y
    ns_ref[...] = u[s - 2:s, :]


def _gated_conv(pc, state, conv_w, n_seq, seq, cb):
    c = conv_w.shape[1]
    ncb = c // cb
    return pl.pallas_call(
        _conv_kernel,
        grid=(n_seq, ncb),
        in_specs=[pl.BlockSpec((seq, cb), lambda b, j: (b, j)),
                  pl.BlockSpec((seq, cb), lambda b, j: (b, ncb + j)),
                  pl.BlockSpec((seq, cb), lambda b, j: (b, 2 * ncb + j)),
                  pl.BlockSpec((None, CONV_K - 1, cb), lambda b, j: (b, 0, j)),
                  pl.BlockSpec((CONV_K, cb), lambda b, j: (0, j))],
        out_specs=[pl.BlockSpec((seq, cb), lambda b, j: (b, j)),
                   pl.BlockSpec((None, CONV_K - 1, cb), lambda b, j: (b, 0, j))],
        out_shape=[jax.ShapeDtypeStruct((n_seq * seq, c), jnp.float32),
                   jax.ShapeDtypeStruct((n_seq, CONV_K - 1, c), jnp.float32)],
        compiler_params=_params(),
        name="gated_conv",
    )(pc, pc, pc, state, conv_w)


def _slope_like(shape, group, head):
    i = (group * ATT_SLOTS + 1 + head).astype(jnp.float32)
    n = float(N_GROUPS * ATT_SLOTS)
    return jnp.exp2(jnp.broadcast_to(-8.0 * i / n, shape))


def _merge_stats(stats):
    ms = [s[0] for s in stats]
    m = functools.reduce(jnp.maximum, ms)
    den = num = None
    for (mg, lg, ng) in stats:
        a = jnp.exp(mg - m)
        den = a * lg if den is None else den + a * lg
        num = a * ng if num is None else num + a * ng
    return num / den


def _prompt_attn_kernel(q0, k0, v0, q1, k1, v1, q2, k2, v2, o_ref, *stat_refs, seq):
    slab = pl.program_id(1)
    lane = lax.broadcasted_iota(jnp.int32, (1, LANES), 1)
    first = lane < HEAD_DIM
    scale = HEAD_DIM ** -0.5

    refs = ((q0, k0, v0), (q1, k1, v1), (q2, k2, v2))
    for g, (win, dil) in enumerate(DIL_PATTERNS):
        q_ref, k_ref, v_ref = refs[g]
        m_ref, l_ref, n_ref = stat_refs[3 * g:3 * g + 3]
        span = dil * N_BACK
        nb = seq // span
        two = nb > 1
        nk = (2 if two else 1) * N_BACK
        qi = lax.broadcasted_iota(jnp.int32, (N_BACK, nk), 0)
        kc = lax.broadcasted_iota(jnp.int32, (N_BACK, nk), 1)
        steps = (N_BACK if two else 0) + qi - kc
        in_band = (steps >= 0) & (steps <= N_BACK)
        dist = (steps * dil).astype(jnp.float32)
        slopes = [_slope_like((1, nk), g, slab * HEADS_PER_SLAB + h)
                  for h in range(HEADS_PER_SLAB)]

        def tile(t, carry, q_ref=q_ref, k_ref=k_ref, v_ref=v_ref, m_ref=m_ref, l_ref=l_ref,
                 n_ref=n_ref, dil=dil, span=span, nb=nb, two=two, kc=kc, in_band=in_band,
                 dist=dist, slopes=slopes):
            r = t // nb
            n = t - r * nb
            start = n * span + r
            rows = pl.ds(start, N_BACK, stride=dil)
            q = q_ref[rows, :] * scale
            if two:
                prev = jnp.maximum(start - span, r)
                prows = pl.ds(prev, N_BACK, stride=dil)
                kk = jnp.concatenate([k_ref[prows, :], k_ref[rows, :]], axis=0)
                vv = jnp.concatenate([v_ref[prows, :], v_ref[rows, :]], axis=0)
                valid = in_band & ((kc >= N_BACK) | (n > 0))
            else:
                kk, vv, valid = k_ref[rows, :], v_ref[rows, :], in_band
            kk = kk.astype(jnp.bfloat16)
            vv = vv.astype(jnp.bfloat16)
            stats = []
            for h in range(HEADS_PER_SLAB):
                mine = first if h == 0 else jnp.logical_not(first)
                qm = jnp.where(mine, q, 0.0).astype(jnp.bfloat16)
                s = lax.dot_general(qm, kk, _NT, preferred_element_type=jnp.float32)
                s = s + jnp.where(valid, -slopes[h] * dist, NEG_INF)
                m = jnp.max(s, axis=-1, keepdims=True)
                p = jnp.exp(s - m)
                l = jnp.sum(p, axis=-1, keepdims=True)
                o = jnp.dot(p.astype(jnp.bfloat16), vv, preferred_element_type=jnp.float32)
                stats.append((m, l, o))
            m_ref[rows, :] = jnp.where(first, stats[0][0], stats[1][0])
            l_ref[rows, :] = jnp.where(first, stats[0][1], stats[1][1])
            n_ref[rows, :] = jnp.where(first, stats[0][2], stats[1][2])
            return carry

        lax.fori_loop(0, dil * nb, tile, 0, unroll=4)

    chunk = 256

    def merge(c, carry):
        rows = pl.ds(pl.multiple_of(c * chunk, chunk), chunk)
        o_ref[rows, :] = _merge_stats([[stat_refs[3 * g + j][rows, :] for j in range(3)]
                                       for g in range(N_GROUPS)])
        return carry

    lax.fori_loop(0, seq // chunk, merge, 0)


def _prompt_attention(q, k, v, n_seq, seq):
    assert seq % (DIL_PATTERNS[-1][1] * N_BACK) == 0
    in_specs = []
    for g in range(N_GROUPS):
        for _ in range(3):
            in_specs.append(pl.BlockSpec((seq, LANES), lambda b, j, g=g: (b, g * N_SLABS + j)))
    args = []
    for g in range(N_GROUPS):
        args += [q, k, v]
    return pl.pallas_call(
        functools.partial(_prompt_attn_kernel, seq=seq),
        grid=(n_seq, N_SLABS),
        in_specs=in_specs,
        out_specs=pl.BlockSpec((seq, LANES), lambda b, j: (b, j)),
        out_shape=jax.ShapeDtypeStruct((n_seq * seq, ATT_WIDTH), jnp.float32),
        scratch_shapes=[pltpu.VMEM((seq, LANES), jnp.float32)] * (3 * N_GROUPS),
        compiler_params=_params(),
        name="prompt_attention",
    )(*args)


SAMPLE_CHUNK_ROWS = 256


def _sample_group_kernel(q_ref, kn_ref, vn_ref, ck_ref, cv_ref, kx_ref, vx_ref, knx_ref, vnx_ref,
                         ok_ref, ov_ref, m_out, l_out, o_out, m_s, l_s, o_s, *, group, dil):
    c = pl.program_id(1)
    last = c == pl.num_programs(1) - 1
    mc = ck_ref.shape[0]
    t_new = q_ref.shape[0]
    scale = HEAD_DIM ** -0.5

    @pl.when(c == 0)
    def _():
        m_s[...] = jnp.full(m_s.shape, NEG_INF, jnp.float32)
        l_s[...] = jnp.zeros(l_s.shape, jnp.float32)
        o_s[...] = jnp.zeros(o_s.shape, jnp.float32)

    for src, nxt, new, dst in ((ck_ref, kx_ref, knx_ref, ok_ref), (cv_ref, vx_ref, vnx_ref, ov_ref)):
        tail = jnp.where(last, new[...], nxt[...])
        if dil <= t_new:
            sh = t_new // dil
            dst[0:mc - sh] = src[sh:mc]
            dst[mc - sh:mc] = tail
        else:
            half = dil - t_new
            dst[:, 0:half] = src[:, t_new:dil]
            dst[0:mc - 1, half:dil] = src[1:mc, 0:t_new]
            dst[mc - 1:mc, half:dil] = tail

    head = lax.broadcasted_iota(jnp.int32, (1, ATT_SLOTS, 1), 1)
    slope = _slope_like((1, ATT_SLOTS, 1), group, head)
    m_idx = c * mc + lax.broadcasted_iota(jnp.int32, (mc, 1, 1), 0)
    for t in range(t_new):
        r, o = t % dil, t // dil
        qt = q_ref[t] * scale
        s = jnp.sum(ck_ref[:, r] * qt, axis=-1, keepdims=True)
        i = m_idx - o
        s = jnp.where(i >= 0, s - slope * ((N_BACK - i) * dil).astype(jnp.float32), NEG_INF)
        m_old = m_s[t]
        m_new = jnp.maximum(m_old, jnp.max(s, axis=0))
        p = jnp.exp(s - m_new)
        a = jnp.exp(m_old - m_new)
        m_s[t] = m_new
        l_s[t] = a * l_s[t] + jnp.sum(p, axis=0)
        o_s[t] = a * o_s[t] + jnp.sum(p * cv_ref[:, r], axis=0)

    @pl.when(last)
    def _():
        slope2 = slope[0]
        for t in range(t_new):
            qt = q_ref[t] * scale
            m_t, l_t, o_t = m_s[t], l_s[t], o_s[t]
            for n in range(t + 1):
                if (t - n) % dil:
                    continue
                s = jnp.sum(kn_ref[n] * qt, axis=-1, keepdims=True) - slope2 * float(t - n)
                m_new = jnp.maximum(m_t, s)
                a = jnp.exp(m_t - m_new)
                p = jnp.exp(s - m_new)
                l_t = a * l_t + p
                o_t = a * o_t + p * vn_ref[n]
                m_t = m_new
            m_out[t] = m_t
            l_out[t] = l_t
            o_out[t] = o_t


def _sample_group(group, q_new, k_new, v_new, cache_k, cache_v):
    win, dil = DIL_PATTERNS[group]
    n, w, nh, dh = cache_k.shape
    t_new = q_new.shape[1]
    assert w == win and w % dil == 0 and (t_new % dil == 0 or dil % t_new == 0)
    nm = w // dil
    mc = min(nm, SAMPLE_CHUNK_ROWS // dil)
    xm, xr = (t_new // dil, dil) if dil <= t_new else (1, t_new)
    view = lambda a: a.reshape(n, nm, dil, nh, dh)
    new_view = lambda a: a.reshape(n, xm, xr, nh, dh)
    row_spec = pl.BlockSpec((None, t_new, nh, dh), lambda b, c: (b, 0, 0, 0))
    chunk_spec = pl.BlockSpec((None, mc, dil, nh, dh), lambda b, c: (b, c, 0, 0, 0))
    next_spec = pl.BlockSpec((None, xm, xr, nh, dh),
                             lambda b, c: (b, jnp.minimum((c + 1) * (mc // xm), nm // xm - 1), 0, 0, 0))
    newx_spec = pl.BlockSpec((None, xm, xr, nh, dh), lambda b, c: (b, 0, 0, 0, 0))
    stat_shape = jax.ShapeDtypeStruct((n, t_new, nh, dh), jnp.float32)
    cache_shape = jax.ShapeDtypeStruct((n, nm, dil, nh, dh), jnp.float32)
    ok, ov, m, l, o = pl.pallas_call(
        functools.partial(_sample_group_kernel, group=group, dil=dil),
        grid=(n, nm // mc),
        in_specs=[row_spec, row_spec, row_spec, chunk_spec, chunk_spec, next_spec, next_spec,
                  newx_spec, newx_spec],
        out_specs=[chunk_spec, chunk_spec, row_spec, row_spec, row_spec],
        out_shape=[cache_shape, cache_shape, stat_shape, stat_shape, stat_shape],
        scratch_shapes=[pltpu.VMEM((t_new, nh, dh), jnp.float32)] * 3,
        compiler_params=_params(),
        name="sample_group",
    )(q_new, k_new, v_new, view(cache_k), view(cache_v), view(cache_k), view(cache_v),
      new_view(k_new), new_view(v_new))
    return ok.reshape(n, w, nh, dh), ov.reshape(n, w, nh, dh), (m, l, o)


def _merge_kernel(*refs):
    stats, o_ref = refs[:-1], refs[-1]
    o_ref[...] = _merge_stats([[stats[3 * g + j][...] for j in range(3)]
                               for g in range(N_GROUPS)])


def _merge_groups(stats):
    flat = [a for s in stats for a in s]
    t, wd = flat[0].shape
    spec = pl.BlockSpec((t, wd), lambda: (0, 0))
    return pl.pallas_call(
        _merge_kernel,
        in_specs=[spec] * len(flat),
        out_specs=spec,
        out_shape=jax.ShapeDtypeStruct((t, wd), jnp.float32),
        compiler_params=_params(),
        name="merge_groups",
    )(*flat)


def _catnorm_kernel(att_ref, cy_ref, ga_ref, gc_ref, o_ref):
    wa = att_ref.shape[1]
    o_ref[:, 0:wa] = _rms(att_ref[...], ga_ref[...]).astype(o_ref.dtype)
    o_ref[:, wa:] = _rms(cy_ref[...], gc_ref[...]).astype(o_ref.dtype)


def _catnorm(att, cy, g_att, g_conv, tm):
    t, wa = att.shape
    wc = cy.shape[1]
    return pl.pallas_call(
        _catnorm_kernel,
        grid=(t // tm,),
        in_specs=[pl.BlockSpec((tm, wa), lambda i: (i, 0)),
                  pl.BlockSpec((tm, wc), lambda i: (i, 0)),
                  pl.BlockSpec((1, wa), lambda i: (0, 0)),
                  pl.BlockSpec((1, wc), lambda i: (0, 0))],
        out_specs=pl.BlockSpec((tm, wa + wc), lambda i: (i, 0)),
        out_shape=jax.ShapeDtypeStruct((t, wa + wc), jnp.bfloat16),
        compiler_params=_params(),
        name="catnorm",
    )(att, cy, g_att.reshape(1, wa), g_conv.reshape(1, wc))


def _outproj_kernel(a_ref, w_ref, x_ref, g_ref, h_ref, hn_ref):
    h = x_ref[...] + jnp.dot(a_ref[...], w_ref[...], preferred_element_type=jnp.float32)
    h_ref[...] = h
    hn_ref[...] = _rms(h, g_ref[...]).astype(hn_ref.dtype)


def _outproj(cat, w_out, x, g_ffn, tm):
    t, k = cat.shape
    d = w_out.shape[1]
    return pl.pallas_call(
        _outproj_kernel,
        grid=(t // tm,),
        in_specs=[pl.BlockSpec((tm, k), lambda i: (i, 0)),
                  pl.BlockSpec((k, d), lambda i: (0, 0)),
                  pl.BlockSpec((tm, d), lambda i: (i, 0)),
                  pl.BlockSpec((1, d), lambda i: (0, 0))],
        out_specs=[pl.BlockSpec((tm, d), lambda i: (i, 0)),
                   pl.BlockSpec((tm, d), lambda i: (i, 0))],
        out_shape=[jax.ShapeDtypeStruct((t, d), jnp.float32),
                   jax.ShapeDtypeStruct((t, d), jnp.bfloat16)],
        compiler_params=_params(),
        name="outproj",
    )(cat, w_out, x, g_ffn.reshape(1, d))


_N_RANK = PEER_TOPK + 1
_CAND_PER_RANK = tuple(_N_RANK // (i + 1) for i in range(_N_RANK))
_N_CAND = sum(_CAND_PER_RANK)
_CAND_ROWS = -(-_N_CAND // 8) * 8


def _top_values(xs, k):
    tt = xs[0].shape[1]
    slot = lax.broadcasted_iota(jnp.int32, (k, tt), 0)
    rowfs = [lax.broadcasted_iota(jnp.int32, x.shape, 0).astype(jnp.float32) for x in xs]

    def step(r, carry):
        new = []
        for (x, vals), rowf in zip(carry, rowfs):
            m = jnp.max(x, axis=0, keepdims=True)
            first = jnp.min(jnp.where(x == m, rowf, float(x.shape[0])), axis=0, keepdims=True)
            new.append((jnp.where(rowf == first, -jnp.inf, x), jnp.where(slot == r, m, vals)))
        return tuple(new)

    init = tuple((x, jnp.zeros((k, tt), jnp.float32)) for x in xs)
    return [vals for _, vals in lax.fori_loop(0, k, step, init)]


def _route_kernel(q_ref, ka_ref, kb_ref, thr_ref, sb_ref, ea_ref, eb_ref):
    ka = ka_ref[...]
    kb = kb_ref[...]
    tt = q_ref.shape[0]
    sub = N_KEYS // 8

    def per_head(h, carry):
        c0 = pl.multiple_of(h * PEER_QDIM, PEER_QDIM)
        qa = q_ref[:, pl.ds(c0, PEER_HALF)]
        qb = q_ref[:, pl.ds(c0 + PEER_HALF, PEER_HALF)]
        sa = lax.dot_general(ka, qa, _NT, preferred_element_type=jnp.float32)
        sb = lax.dot_general(kb, qb, _NT, preferred_element_type=jnp.float32)
        va, vb = _top_values((sa, sb), _N_RANK)
        pieces = [va[i:i + 1, :] + vb[0:n, :] for i, n in enumerate(_CAND_PER_RANK)]
        pieces.append(jnp.full((_CAND_ROWS - _N_CAND, tt), -jnp.inf, jnp.float32))
        (top,) = _top_values((jnp.concatenate(pieces, axis=0),), _N_RANK)
        z = jnp.sum(jnp.exp(top[0:PEER_TOPK, :] - top[0:1, :]), axis=0, keepdims=True)
        tau = 0.5 * (top[PEER_TOPK - 1:PEER_TOPK, :] + top[PEER_TOPK:PEER_TOPK + 1, :])
        thr = tau - sa
        ea = jnp.exp(sa - va[0:1, :])
        eb = jnp.exp(sb - vb[0:1, :]) / z
        for lt in range(tt // LANES):
            ls = slice(lt * LANES, (lt + 1) * LANES)
            thr_ref[h, lt] = thr[:, ls].reshape(sub, 8, LANES)
            ea_ref[h, lt] = ea[:, ls].reshape(sub, 8, LANES)
            sb_ref[h, lt] = sb[:, ls]
            eb_ref[h, lt] = eb[:, ls]
        return carry

    lax.fori_loop(0, PEER_HEADS, per_head, 0)


def _route_specs(tt, imap):
    n_lt = tt // LANES
    row = pl.BlockSpec((PEER_HEADS, n_lt, N_KEYS // 8, 8, LANES), lambda *g: (0, imap(*g), 0, 0, 0))
    full = pl.BlockSpec((PEER_HEADS, n_lt, N_KEYS, LANES), lambda *g: (0, imap(*g), 0, 0))
    return [row, full, row, full]


def _route(qp, ka, kb, tt):
    t = qp.shape[0]
    row_shape = jax.ShapeDtypeStruct((PEER_HEADS, t // LANES, N_KEYS // 8, 8, LANES), jnp.float32)
    full_shape = jax.ShapeDtypeStruct((PEER_HEADS, t // LANES, N_KEYS, LANES), jnp.float32)
    return pl.pallas_call(
        _route_kernel,
        grid=(t // tt,),
        in_specs=[pl.BlockSpec((tt, PEER_HEADS * PEER_QDIM), lambda i: (i, 0)),
                  pl.BlockSpec((N_KEYS, PEER_HALF), lambda i: (0, 0)),
                  pl.BlockSpec((N_KEYS, PEER_HALF), lambda i: (0, 0))],
        out_specs=_route_specs(tt, lambda i: i),
        out_shape=[row_shape, full_shape, row_shape, full_shape],
        compiler_params=_params(),
        name="peer_route",
    )(qp, ka, kb)


def _gelu(x):
    return 0.5 * x * (1.0 + lax.erf(x * math.sqrt(0.5)))


def _peer_kernel(hn_ref, u_ref, v_ref, thr_ref, sb_ref, ea_ref, eb_ref, o_ref, h_scr, w_scr):
    e = pl.program_id(1)
    et, tt = h_scr.shape
    assert et == 8 * N_KEYS

    @pl.when(e == 0)
    def _():
        o_ref[...] = jnp.zeros(o_ref.shape, o_ref.dtype)

    h_scr[...] = lax.dot_general(u_ref[...], hn_ref[...], _NT,
                                 preferred_element_type=jnp.float32)

    def per_lane_tile(lt, carry):
        ls = pl.ds(pl.multiple_of(lt * LANES, LANES), LANES)
        for a in range(et // N_KEYS):
            rows = pl.ds(a * N_KEYS, N_KEYS)
            acc = None
            for h in range(PEER_HEADS):
                keep = sb_ref[h, lt] >= thr_ref[h, lt, e, a:a + 1, :]
                gate = jnp.where(keep, eb_ref[h, lt], 0.0) * ea_ref[h, lt, e, a:a + 1, :]
                acc = gate if acc is None else acc + gate
            w_scr[rows, ls] = (acc * _gelu(h_scr[rows, ls])).astype(w_scr.dtype)
        return carry

    lax.fori_loop(0, tt // LANES, per_lane_tile, 0)
    o_ref[...] += lax.dot_general(w_scr[...], v_ref[...], _TN,
                                  preferred_element_type=jnp.float32)


def _peer_dense(hn, u, v, routing, tt, et):
    t, d = hn.shape
    n_exp = u.shape[0]
    return pl.pallas_call(
        _peer_kernel,
        grid=(t // tt, n_exp // et),
        in_specs=[pl.BlockSpec((tt, d), lambda i, e: (i, 0)),
                  pl.BlockSpec((et, d), lambda i, e: (e, 0)),
                  pl.BlockSpec((et, d), lambda i, e: (e, 0))] + _route_specs(tt, lambda i, e: i),
        out_specs=pl.BlockSpec((tt, d), lambda i, e: (i, 0)),
        out_shape=jax.ShapeDtypeStruct((t, d), jnp.float32),
        scratch_shapes=[pltpu.VMEM((et, tt), jnp.float32),
                        pltpu.VMEM((et, tt), jnp.bfloat16)],
        compiler_params=_params(),
        name="peer_dense",
    )(hn, u, v, *routing)


def _final_kernel(h_ref, p_ref, g_ref, o_ref):
    o_ref[...] = _rms(h_ref[...] + p_ref[...], g_ref[...])


def _final(h, p, g, tm):
    t, d = h.shape
    row = pl.BlockSpec((tm, d), lambda i: (i, 0))
    return pl.pallas_call(
        _final_kernel,
        grid=(t // tm,),
        in_specs=[row, row, pl.BlockSpec((1, d), lambda i: (0, 0))],
        out_specs=row,
        out_shape=jax.ShapeDtypeStruct((t, d), jnp.float32),
        compiler_params=_params(),
        name="final_norm",
    )(h, p, g.reshape(1, d))


def _tile(t, pref):
    while t % pref:
        pref //= 2
    return pref


def _project_in(x2, norm_g, w_in_bf):
    t = x2.shape[0]
    xn = _rmsnorm(x2, norm_g, jnp.bfloat16, _tile(t, 256))
    tm = _tile(t, 1024)
    cw = (w_in_bf.shape[1] - 3 * QKV_WIDTH)
    q = _matmul(xn, w_in_bf, 0, QKV_WIDTH, jnp.float32, tm, 1024)
    k = _matmul(xn, w_in_bf, QKV_WIDTH, QKV_WIDTH, jnp.float32, tm, 1024)
    v = _matmul(xn, w_in_bf, 2 * QKV_WIDTH, QKV_WIDTH, jnp.float32, tm, 1024)
    pc = _matmul(xn, w_in_bf, 3 * QKV_WIDTH, cw, jnp.float32, tm, 1024)
    return q, k, v, pc


def _channel_mix(x2, att, cy, lw):
    t = x2.shape[0]
    cat = _catnorm(att, cy, lw["g_att"], lw["g_conv"], _tile(t, 256))
    h, hn = _outproj(cat, lw["w_out"], x2, lw["g_ffn"], _tile(t, 256))
    qp = _matmul(hn, lw["w_pq"], 0, lw["w_pq"].shape[1], jnp.bfloat16, _tile(t, 1024), 1024)
    routing = _route(qp, lw["ka"], lw["kb"], _tile(t, 256))
    peer = _peer_dense(hn, lw["u"], lw["v"], routing, _tile(t, 512), 8 * N_KEYS)
    return h, peer


def kernel(x_prompt, x_sample, cache_k_g0, cache_v_g0, cache_k_g1, cache_v_g1, cache_k_g2, cache_v_g2, state_conv, norm_mix, w_in, conv_w, norm_att_out, norm_conv_out, w_out, norm_ffn, w_pq, sub_keys_a, sub_keys_b, expert_u, expert_v, norm_final):
    bf = jnp.bfloat16
    depth = w_in.shape[0]
    b, s, d = x_prompt.shape
    bd, td, _ = x_sample.shape
    caches_k = (cache_k_g0, cache_k_g1, cache_k_g2)
    caches_v = (cache_v_g0, cache_v_g1, cache_v_g2)
    conv_c = conv_w.shape[2]

    hp = x_prompt.reshape(b * s, d)
    hs = x_sample.reshape(bd * td, d)
    nk_p = [[] for _ in range(N_GROUPS)]
    nv_p = [[] for _ in range(N_GROUPS)]
    nk_s = [[] for _ in range(N_GROUPS)]
    nv_s = [[] for _ in range(N_GROUPS)]
    nc_p, nc_s = [], []
    for l in range(depth):
        lw = dict(g_att=norm_att_out[l], g_conv=norm_conv_out[l], w_out=w_out[l].astype(bf),
                  g_ffn=norm_ffn[l], w_pq=w_pq[l].astype(bf), ka=sub_keys_a[l].astype(bf),
                  kb=sub_keys_b[l].astype(bf), u=expert_u[l].astype(bf), v=expert_v[l].astype(bf))
        w_in_bf = w_in[l].astype(bf)

        q, k, v, pc = _project_in(hp, norm_mix[l], w_in_bf)
        att = _prompt_attention(q, k, v, b, s)
        cy, ns = _gated_conv(pc, jnp.zeros((b, CONV_K - 1, conv_c), jnp.float32), conv_w[l], b, s, 256)
        k5 = k.reshape(b, s, N_GROUPS, ATT_SLOTS, HEAD_DIM)
        v5 = v.reshape(b, s, N_GROUPS, ATT_SLOTS, HEAD_DIM)
        for g, (win, _) in enumerate(DIL_PATTERNS):
            rows = min(win, s)
            nk_p[g].append(k5[:, s - rows:, g])
            nv_p[g].append(v5[:, s - rows:, g])
        nc_p.append(ns)
        h_res, peer = _channel_mix(hp, att, cy, lw)
        last = l == depth - 1
        if last:
            y_prompt = _final(h_res, peer, norm_final, _tile(b * s, 256))
        else:
            hp = h_res + peer

        q, k, v, pc = _project_in(hs, norm_mix[l], w_in_bf)
        split = lambda a: a.reshape(bd, td, N_GROUPS, ATT_SLOTS, HEAD_DIM)
        q5, k5, v5 = split(q), split(k), split(v)
        stats = []
        for g in range(N_GROUPS):
            ok, ov, st = _sample_group(g, q5[:, :, g], k5[:, :, g], v5[:, :, g],
                                       caches_k[g][l], caches_v[g][l])
            nk_s[g].append(ok)
            nv_s[g].append(ov)
            stats.append([a.reshape(bd * td, ATT_WIDTH) for a in st])
        att = _merge_groups(stats)
        cy, ns = _gated_conv(pc, state_conv[l], conv_w[l], bd, td, conv_c)
        nc_s.append(ns)
        h_res, peer = _channel_mix(hs, att, cy, lw)
        if last:
            y_sample = _final(h_res, peer, norm_final, _tile(bd * td, 256))
        else:
            hs = h_res + peer

    return (y_prompt.reshape(b, s, d), y_sample.reshape(bd, td, d),
            jnp.stack(nk_p[0], 0), jnp.stack(nv_p[0], 0),
            jnp.stack(nk_p[1], 0), jnp.stack(nv_p[1], 0),
            jnp.stack(nk_p[2], 0), jnp.stack(nv_p[2], 0),
            jnp.stack(nc_p, 0),
            jnp.stack(nk_s[0], 0), jnp.stack(nv_s[0], 0),
            jnp.stack(nk_s[1], 0), jnp.stack(nv_s[1], 0),
            jnp.stack(nk_s[2], 0), jnp.stack(nv_s[2], 0),
            jnp.stack(nc_s, 0))
```

```python
import functools
import math

import jax
import jax.numpy as jnp
from jax import lax
from jax.experimental import pallas as pl
from jax.experimental.pallas import tpu as pltpu

HEAD_DIM = 64
ATT_SLOTS = 16
DIL_PATTERNS = ((128, 1), (512, 4), (2048, 16))
N_GROUPS = len(DIL_PATTERNS)
ATT_WIDTH = ATT_SLOTS * HEAD_DIM
QKV_WIDTH = N_GROUPS * ATT_WIDTH
CONV_K = 3
N_KEYS = 128
PEER_HEADS = 8
PEER_QDIM = 256
PEER_HALF = PEER_QDIM // 2
PEER_TOPK = 16
NORM_EPS = 1e-6
NEG_INF = -1e30

LANES = 128
HEADS_PER_SLAB = LANES // HEAD_DIM
N_SLABS = ATT_WIDTH // LANES
N_BACK = 128
VMEM_LIMIT = 56 * 1024 * 1024

_NT = (((1,), (1,)), ((), ()))
_TN = (((0,), (0,)), ((), ()))


def _params(vmem=None):
    return pltpu.CompilerParams(vmem_limit_bytes=vmem or VMEM_LIMIT)


def _rms(x, g):
    return x * lax.rsqrt(jnp.mean(x * x, axis=-1, keepdims=True) + NORM_EPS) * g


def _rmsnorm_kernel(x_ref, g_ref, o_ref):
    o_ref[...] = _rms(x_ref[...], g_ref[...]).astype(o_ref.dtype)


def _rmsnorm(x, g, out_dtype, tm):
    t, d = x.shape
    return pl.pallas_call(
        _rmsnorm_kernel,
        grid=(t // tm,),
        in_specs=[pl.BlockSpec((tm, d), lambda i: (i, 0)),
                  pl.BlockSpec((1, d), lambda i: (0, 0))],
        out_specs=pl.BlockSpec((tm, d), lambda i: (i, 0)),
        out_shape=jax.ShapeDtypeStruct((t, d), out_dtype),
        compiler_params=_params(),
        name="rmsnorm",
    )(x, g.reshape(1, d))


def _mm_kernel(a_ref, b_ref, o_ref):
    o_ref[...] = jnp.dot(a_ref[...], b_ref[...],
                         preferred_element_type=jnp.float32).astype(o_ref.dtype)


def _matmul(a, b, col_start, n_cols, out_dtype, tm, tn):
    t, k = a.shape
    off = col_start // tn
    return pl.pallas_call(
        _mm_kernel,
        grid=(t // tm, n_cols // tn),
        in_specs=[pl.BlockSpec((tm, k), lambda i, j: (i, 0)),
                  pl.BlockSpec((k, tn), lambda i, j: (0, j + off))],
        out_specs=pl.BlockSpec((tm, tn), lambda i, j: (i, j)),
        out_shape=jax.ShapeDtypeStruct((t, n_cols), out_dtype),
        compiler_params=_params(),
        name="matmul",
    )(a, b)


def _mm_t_kernel(w_ref, x_ref, o_ref):
    o_ref[...] = lax.dot_general(w_ref[...], x_ref[...], (((0,), (1,)), ((), ())),
                                 preferred_element_type=jnp.float32)


def _matmul_t(w, x, col_start, n_cols, n_seq, seq, rows, tn):
    k = x.shape[1]
    per = seq // rows
    off = col_start // tn
    return pl.pallas_call(
        _mm_t_kernel,
        grid=(n_seq, n_cols // tn),
        in_specs=[pl.BlockSpec((k, tn), lambda b, j: (0, j + off)),
                  pl.BlockSpec((rows, k), lambda b, j: (b * per + per - 1, 0))],
        out_specs=pl.BlockSpec((None, tn, rows), lambda b, j: (b, j, 0)),
        out_shape=jax.ShapeDtypeStruct((n_seq, n_cols, rows), jnp.float32),
        compiler_params=_params(),
        name="matmul_t",
    )(w, x)


def _conv_kernel(gb_ref, gc_ref, h_ref, st_ref, w_ref, y_ref, ns_ref):
    u = gc_ref[...] * h_ref[...]
    s = u.shape[0]
    row = lax.broadcasted_iota(jnp.int32, u.shape, 0)
    st = st_ref[...]
    w = w_ref[...]
    u1 = jnp.where(row == 0, st[1:2, :], pltpu.roll(u, 1, 0))
    u2 = jnp.where(row == 0, st[0:1, :],
                   jnp.where(row == 1, st[1:2, :], pltpu.roll(u, 2, 0)))
    y = u2 * w[0:1, :]
    y = y + u1 * w[1:2, :]
    y = y + u * w[2:3, :]
    y_ref[...] = gb_ref[...] * y
    ns_ref[...] = u[s - 2:s, :]


def _gated_conv(pc, state, conv_w, n_seq, seq, cb):
    c = conv_w.shape[1]
    ncb = c // cb
    return pl.pallas_call(
        _conv_kernel,
        grid=(n_seq, ncb),
        in_specs=[pl.BlockSpec((seq, cb), lambda b, j: (b, j)),
                  pl.BlockSpec((seq, cb), lambda b, j: (b, ncb + j)),
                  pl.BlockSpec((seq, cb), lambda b, j: (b, 2 * ncb + j)),
                  pl.BlockSpec((None, CONV_K - 1, cb), lambda b, j: (b, 0, j)),
                  pl.BlockSpec((CONV_K, cb), lambda b, j: (0, j))],
        out_specs=[pl.BlockSpec((seq, cb), lambda b, j: (b, j)),
                   pl.BlockSpec((None, CONV_K - 1, cb), lambda b, j: (b, 0, j))],
        out_shape=[jax.ShapeDtypeStruct((n_seq * seq, c), jnp.float32),
                   jax.ShapeDtypeStruct((n_seq, CONV_K - 1, c), jnp.float32)],
        compiler_params=_params(),
        name="gated_conv",
    )(pc, pc, pc, state, conv_w)


def _slope_like(shape, group, head):
    i = (group * ATT_SLOTS + 1 + head).astype(jnp.float32)
    n = float(N_GROUPS * ATT_SLOTS)
    return jnp.exp2(jnp.broadcast_to(-8.0 * i / n, shape))


def _merge_stats(stats):
    ms = [s[0] for s in stats]
    m = functools.reduce(jnp.maximum, ms)
    den = num = None
    for (mg, lg, ng) in stats:
        a = jnp.exp(mg - m)
        den = a * lg if den is None else den + a * lg
        num = a * ng if num is None else num + a * ng
    return num / den


def _prompt_attn_kernel(q0, k0, v0, q1, k1, v1, q2, k2, v2, o_ref, *stat_refs, seq):
    slab = pl.program_id(1)
    lane = lax.broadcasted_iota(jnp.int32, (1, LANES), 1)
    first = lane < HEAD_DIM
    scale = HEAD_DIM ** -0.5

    refs = ((q0, k0, v0), (q1, k1, v1), (q2, k2, v2))
    for g, (win, dil) in enumerate(DIL_PATTERNS):
        q_ref, k_ref, v_ref = refs[g]
        m_ref, l_ref, n_ref = stat_refs[3 * g:3 * g + 3]
        span = dil * N_BACK
        nb = seq // span
        two = nb > 1
        nk = (2 if two else 1) * N_BACK
        qi = lax.broadcasted_iota(jnp.int32, (N_BACK, nk), 0)
        kc = lax.broadcasted_iota(jnp.int32, (N_BACK, nk), 1)
        steps = (N_BACK if two else 0) + qi - kc
        in_band = (steps >= 0) & (steps <= N_BACK)
        dist = (steps * dil).astype(jnp.float32)
        slopes = [_slope_like((1, nk), g, slab * HEADS_PER_SLAB + h)
                  for h in range(HEADS_PER_SLAB)]

        def tile(t, carry, q_ref=q_ref, k_ref=k_ref, v_ref=v_ref, m_ref=m_ref, l_ref=l_ref,
                 n_ref=n_ref, dil=dil, span=span, nb=nb, two=two, kc=kc, in_band=in_band,
                 dist=dist, slopes=slopes):
            r = t // nb
            n = t - r * nb
            start = n * span + r
            rows = pl.ds(start, N_BACK, stride=dil)
            q = q_ref[rows, :] * scale
            if two:
                prev = jnp.maximum(start - span, r)
                prows = pl.ds(prev, N_BACK, stride=dil)
                kk = jnp.concatenate([k_ref[prows, :], k_ref[rows, :]], axis=0)
                vv = jnp.concatenate([v_ref[prows, :], v_ref[rows, :]], axis=0)
                valid = in_band & ((kc >= N_BACK) | (n > 0))
            else:
                kk, vv, valid = k_ref[rows, :], v_ref[rows, :], in_band
            kk = kk.astype(jnp.bfloat16)
            vv = vv.astype(jnp.bfloat16)
            stats = []
            for h in range(HEADS_PER_SLAB):
                mine = first if h == 0 else jnp.logical_not(first)
                qm = jnp.where(mine, q, 0.0).astype(jnp.bfloat16)
                s = lax.dot_general(qm, kk, _NT, preferred_element_type=jnp.float32)
                s = s + jnp.where(valid, -slopes[h] * dist, NEG_INF)
                m = jnp.max(s, axis=-1, keepdims=True)
                p = jnp.exp(s - m)
                l = jnp.sum(p, axis=-1, keepdims=True)
                o = jnp.dot(p.astype(jnp.bfloat16), vv, preferred_element_type=jnp.float32)
                stats.append((m, l, o))
            m_ref[rows, :] = jnp.where(first, stats[0][0], stats[1][0])
            l_ref[rows, :] = jnp.where(first, stats[0][1], stats[1][1])
            n_ref[rows, :] = jnp.where(first, stats[0][2], stats[1][2])
            return carry

        lax.fori_loop(0, dil * nb, tile, 0, unroll=4)

    chunk = 256

    def merge(c, carry):
        rows = pl.ds(pl.multiple_of(c * chunk, chunk), chunk)
        o_ref[rows, :] = _merge_stats([[stat_refs[3 * g + j][rows, :] for j in range(3)]
                                       for g in range(N_GROUPS)])
        return carry

    lax.fori_loop(0, seq // chunk, merge, 0)


def _prompt_attention(q, k, v, n_seq, seq):
    assert seq % (DIL_PATTERNS[-1][1] * N_BACK) == 0
    in_specs = []
    for g in range(N_GROUPS):
        for _ in range(3):
            in_specs.append(pl.BlockSpec((seq, LANES), lambda b, j, g=g: (b, g * N_SLABS + j)))
    args = []
    for g in range(N_GROUPS):
        args += [q, k, v]
    return pl.pallas_call(
        functools.partial(_prompt_attn_kernel, seq=seq),
        grid=(n_seq, N_SLABS),
        in_specs=in_specs,
        out_specs=pl.BlockSpec((seq, LANES), lambda b, j: (b, j)),
        out_shape=jax.ShapeDtypeStruct((n_seq * seq, ATT_WIDTH), jnp.float32),
        scratch_shapes=[pltpu.VMEM((seq, LANES), jnp.float32)] * (3 * N_GROUPS),
        compiler_params=_params(),
        name="prompt_attention",
    )(*args)


SAMPLE_HEADS_PER_STEP = 4


def _sample_kernel(*refs, t_new):
    qn_ref, kn_ref, vn_ref = refs[:3]
    cache_refs = refs[3:3 + 2 * N_GROUPS]
    out_refs = refs[3 + 2 * N_GROUPS:3 + 4 * N_GROUPS]
    o_ref = refs[3 + 4 * N_GROUPS]
    hb = o_ref.shape[0]
    scale = HEAD_DIM ** -0.5
    lane = lax.broadcasted_iota(jnp.int32, (1, LANES), 1)
    trow = lax.broadcasted_iota(jnp.int32, (t_new, 1), 0)

    def widen(x):
        return jnp.concatenate([x, jnp.zeros((x.shape[0], LANES - t_new), x.dtype)], axis=1)

    def per_head(h, carry):
        head = pl.program_id(1) * hb + h
        score_tiles, value_tiles = [], []
        for g, (win, dil) in enumerate(DIL_PATTERNS):
            ck, cv = cache_refs[2 * g], cache_refs[2 * g + 1]
            n_tiles = win // LANES
            q_t = qn_ref[g, h] * scale
            k_new = widen(kn_ref[g, h])
            slope = _slope_like((1, LANES), g, head)
            tiles = [jnp.zeros((t_new, LANES), jnp.float32) for _ in range(n_tiles + 1)]
            for t in range(t_new):
                q_col = jnp.broadcast_to(q_t[:, t:t + 1], (HEAD_DIM, LANES))
                for j in range(n_tiles + 1):
                    kt = ck[h, :, j * LANES:(j + 1) * LANES] if j < n_tiles else k_new
                    row = jnp.sum(kt * q_col, axis=0, keepdims=True)
                    tiles[j] = jnp.where(trow == t, row, tiles[j])
            for j in range(n_tiles + 1):
                if j < n_tiles:
                    dist = win + trow - (j * LANES + lane)
                    valid = ((dist & (dil - 1)) == 0) & (dist <= win)
                else:
                    dist = trow - lane
                    valid = ((dist & (dil - 1)) == 0) & (dist >= 0)
                tiles[j] = jnp.where(valid, tiles[j] - slope * dist.astype(jnp.float32), NEG_INF)
            score_tiles += tiles
            value_tiles += [(cv, j) for j in range(n_tiles)] + [(vn_ref, g)]
        m = functools.reduce(jnp.maximum, [jnp.max(s, axis=1, keepdims=True) for s in score_tiles])
        probs = [jnp.exp(s - m) for s in score_tiles]
        denom = functools.reduce(lambda a, b: a + b, [jnp.sum(p, axis=1, keepdims=True) for p in probs])
        probs = [p / denom for p in probs]
        out = jnp.zeros((HEAD_DIM, LANES), jnp.float32)
        for t in range(t_new):
            acc = None
            for p, (ref, j) in zip(probs, value_tiles):
                vt = widen(ref[j, h]) if ref is vn_ref else ref[h, :, j * LANES:(j + 1) * LANES]
                term = vt * p[t:t + 1, :]
                acc = term if acc is None else acc + term
            out = jnp.where(lane == t, jnp.sum(acc, axis=1, keepdims=True), out)
        o_ref[h] = out[:, 0:t_new]
        for g, (win, dil) in enumerate(DIL_PATTERNS):
            for src, new, dst in ((cache_refs[2 * g], kn_ref, out_refs[2 * g]),
                                  (cache_refs[2 * g + 1], vn_ref, out_refs[2 * g + 1])):
                rolled = pltpu.roll(src[h], win - t_new, 1)
                top = pltpu.roll(widen(new[g, h]), LANES - t_new, 1)
                if win > LANES:
                    dst[h, :, 0:win - LANES] = rolled[:, 0:win - LANES]
                dst[h, :, win - LANES:win] = jnp.where(lane >= LANES - t_new, top,
                                                       rolled[:, win - LANES:win])
        return carry

    lax.fori_loop(0, hb, per_head, 0)


def _sample_attention(q_new, k_new, v_new, caches_k, caches_v):
    n, _, nh, dh, t_new = q_new.shape
    hb = SAMPLE_HEADS_PER_STEP
    new_spec = pl.BlockSpec((None, N_GROUPS, hb, dh, t_new), lambda b, j: (b, 0, j, 0, 0))
    in_specs, args, out_specs, out_shape = [new_spec] * 3, [q_new, k_new, v_new], [], []
    for g, (win, dil) in enumerate(DIL_PATTERNS):
        assert caches_k[g].shape == (n, nh, dh, win) and win % LANES == 0 and dil & (dil - 1) == 0
        spec = pl.BlockSpec((None, hb, dh, win), lambda b, j: (b, j, 0, 0))
        in_specs += [spec, spec]
        args += [caches_k[g], caches_v[g]]
        out_specs += [spec, spec]
        out_shape += [jax.ShapeDtypeStruct(caches_k[g].shape, jnp.float32)] * 2
    out_specs.append(pl.BlockSpec((None, hb, dh, t_new), lambda b, j: (b, j, 0, 0)))
    out_shape.append(jax.ShapeDtypeStruct((n, nh, dh, t_new), jnp.float32))
    res = pl.pallas_call(
        functools.partial(_sample_kernel, t_new=t_new),
        grid=(n, nh // hb),
        in_specs=in_specs,
        out_specs=out_specs,
        out_shape=out_shape,
        compiler_params=_params(),
        name="sample_attention",
    )(*args)
    return res[-1], res[:-1]


def _catnorm_kernel(att_ref, cy_ref, ga_ref, gc_ref, o_ref):
    wa = att_ref.shape[1]
    o_ref[:, 0:wa] = _rms(att_ref[...], ga_ref[...]).astype(o_ref.dtype)
    o_ref[:, wa:] = _rms(cy_ref[...], gc_ref[...]).astype(o_ref.dtype)


def _catnorm(att, cy, g_att, g_conv, tm):
    t, wa = att.shape
    wc = cy.shape[1]
    return pl.pallas_call(
        _catnorm_kernel,
        grid=(t // tm,),
        in_specs=[pl.BlockSpec((tm, wa), lambda i: (i, 0)),
                  pl.BlockSpec((tm, wc), lambda i: (i, 0)),
                  pl.BlockSpec((1, wa), lambda i: (0, 0)),
                  pl.BlockSpec((1, wc), lambda i: (0, 0))],
        out_specs=pl.BlockSpec((tm, wa + wc), lambda i: (i, 0)),
        out_shape=jax.ShapeDtypeStruct((t, wa + wc), jnp.bfloat16),
        compiler_params=_params(),
        name="catnorm",
    )(att, cy, g_att.reshape(1, wa), g_conv.reshape(1, wc))


def _outproj_kernel(a_ref, w_ref, x_ref, g_ref, h_ref, hn_ref):
    h = x_ref[...] + jnp.dot(a_ref[...], w_ref[...], preferred_element_type=jnp.float32)
    h_ref[...] = h
    hn_ref[...] = _rms(h, g_ref[...]).astype(hn_ref.dtype)


def _outproj(cat, w_out, x, g_ffn, tm):
    t, k = cat.shape
    d = w_out.shape[1]
    return pl.pallas_call(
        _outproj_kernel,
        grid=(t // tm,),
        in_specs=[pl.BlockSpec((tm, k), lambda i: (i, 0)),
                  pl.BlockSpec((k, d), lambda i: (0, 0)),
                  pl.BlockSpec((tm, d), lambda i: (i, 0)),
                  pl.BlockSpec((1, d), lambda i: (0, 0))],
        out_specs=[pl.BlockSpec((tm, d), lambda i: (i, 0)),
                   pl.BlockSpec((tm, d), lambda i: (i, 0))],
        out_shape=[jax.ShapeDtypeStruct((t, d), jnp.float32),
                   jax.ShapeDtypeStruct((t, d), jnp.bfloat16)],
        compiler_params=_params(),
        name="outproj",
    )(cat, w_out, x, g_ffn.reshape(1, d))


_N_RANK = PEER_TOPK + 1
_CAND_PER_RANK = tuple(_N_RANK // (i + 1) for i in range(_N_RANK))
_N_CAND = sum(_CAND_PER_RANK)
_CAND_ROWS = -(-_N_CAND // 8) * 8


def _top_values(xs, k):
    tt = xs[0].shape[1]
    slot = lax.broadcasted_iota(jnp.int32, (k, tt), 0)
    rowfs = [lax.broadcasted_iota(jnp.int32, x.shape, 0).astype(jnp.float32) for x in xs]

    def step(r, carry):
        new = []
        for (x, vals), rowf in zip(carry, rowfs):
            m = jnp.max(x, axis=0, keepdims=True)
            first = jnp.min(jnp.where(x == m, rowf, float(x.shape[0])), axis=0, keepdims=True)
            new.append((jnp.where(rowf == first, -jnp.inf, x), jnp.where(slot == r, m, vals)))
        return tuple(new)

    init = tuple((x, jnp.zeros((k, tt), jnp.float32)) for x in xs)
    return [vals for _, vals in lax.fori_loop(0, k, step, init)]


def _route_kernel(q_ref, ka_ref, kb_ref, thr_ref, sb_ref, ea_ref, eb_ref):
    ka = ka_ref[...]
    kb = kb_ref[...]
    tt = q_ref.shape[0]
    sub = N_KEYS // 8

    def per_head(h, carry):
        c0 = pl.multiple_of(h * PEER_QDIM, PEER_QDIM)
        qa = q_ref[:, pl.ds(c0, PEER_HALF)]
        qb = q_ref[:, pl.ds(c0 + PEER_HALF, PEER_HALF)]
        sa = lax.dot_general(ka, qa, _NT, preferred_element_type=jnp.float32)
        sb = lax.dot_general(kb, qb, _NT, preferred_element_type=jnp.float32)
        va, vb = _top_values((sa, sb), _N_RANK)
        pieces = [va[i:i + 1, :] + vb[0:n, :] for i, n in enumerate(_CAND_PER_RANK)]
        pieces.append(jnp.full((_CAND_ROWS - _N_CAND, tt), -jnp.inf, jnp.float32))
        (top,) = _top_values((jnp.concatenate(pieces, axis=0),), _N_RANK)
        z = jnp.sum(jnp.exp(top[0:PEER_TOPK, :] - top[0:1, :]), axis=0, keepdims=True)
        tau = 0.5 * (top[PEER_TOPK - 1:PEER_TOPK, :] + top[PEER_TOPK:PEER_TOPK + 1, :])
        thr = tau - sa
        ea = jnp.exp(sa - va[0:1, :])
        eb = jnp.exp(sb - vb[0:1, :]) / z
        for lt in range(tt // LANES):
            ls = slice(lt * LANES, (lt + 1) * LANES)
            thr_ref[h, lt] = thr[:, ls].reshape(sub, 8, LANES)
            ea_ref[h, lt] = ea[:, ls].reshape(sub, 8, LANES)
            sb_ref[h, lt] = sb[:, ls]
            eb_ref[h, lt] = eb[:, ls]
        return carry

    lax.fori_loop(0, PEER_HEADS, per_head, 0)


def _route_specs(tt, imap):
    n_lt = tt // LANES
    row = pl.BlockSpec((PEER_HEADS, n_lt, N_KEYS // 8, 8, LANES), lambda *g: (0, imap(*g), 0, 0, 0))
    full = pl.BlockSpec((PEER_HEADS, n_lt, N_KEYS, LANES), lambda *g: (0, imap(*g), 0, 0))
    return [row, full, row, full]


def _route(qp, ka, kb, tt):
    t = qp.shape[0]
    row_shape = jax.ShapeDtypeStruct((PEER_HEADS, t // LANES, N_KEYS // 8, 8, LANES), jnp.float32)
    full_shape = jax.ShapeDtypeStruct((PEER_HEADS, t // LANES, N_KEYS, LANES), jnp.float32)
    return pl.pallas_call(
        _route_kernel,
        grid=(t // tt,),
        in_specs=[pl.BlockSpec((tt, PEER_HEADS * PEER_QDIM), lambda i: (i, 0)),
                  pl.BlockSpec((N_KEYS, PEER_HALF), lambda i: (0, 0)),
                  pl.BlockSpec((N_KEYS, PEER_HALF), lambda i: (0, 0))],
        out_specs=_route_specs(tt, lambda i: i),
        out_shape=[row_shape, full_shape, row_shape, full_shape],
        compiler_params=_params(),
        name="peer_route",
    )(qp, ka, kb)


def _gelu(x):
    return 0.5 * x * (1.0 + lax.erf(x * math.sqrt(0.5)))


def _peer_kernel(hn_ref, u_ref, v_ref, thr_ref, sb_ref, ea_ref, eb_ref, o_ref, h_scr, w_scr):
    e = pl.program_id(1)
    et, tt = h_scr.shape
    assert et == 8 * N_KEYS

    @pl.when(e == 0)
    def _():
        o_ref[...] = jnp.zeros(o_ref.shape, o_ref.dtype)

    h_scr[...] = lax.dot_general(u_ref[...], hn_ref[...], _NT,
                                 preferred_element_type=jnp.float32)

    def per_lane_tile(lt, carry):
        ls = pl.ds(pl.multiple_of(lt * LANES, LANES), LANES)
        for a in range(et // N_KEYS):
            rows = pl.ds(a * N_KEYS, N_KEYS)
            acc = None
            for h in range(PEER_HEADS):
                keep = sb_ref[h, lt] >= thr_ref[h, lt, e, a:a + 1, :]
                gate = jnp.where(keep, eb_ref[h, lt], 0.0) * ea_ref[h, lt, e, a:a + 1, :]
                acc = gate if acc is None else acc + gate
            w_scr[rows, ls] = (acc * _gelu(h_scr[rows, ls])).astype(w_scr.dtype)
        return carry

    lax.fori_loop(0, tt // LANES, per_lane_tile, 0)
    o_ref[...] += lax.dot_general(w_scr[...], v_ref[...], _TN,
                                  preferred_element_type=jnp.float32)


def _peer_dense(hn, u, v, routing, tt, et):
    t, d = hn.shape
    n_exp = u.shape[0]
    return pl.pallas_call(
        _peer_kernel,
        grid=(t // tt, n_exp // et),
        in_specs=[pl.BlockSpec((tt, d), lambda i, e: (i, 0)),
                  pl.BlockSpec((et, d), lambda i, e: (e, 0)),
                  pl.BlockSpec((et, d), lambda i, e: (e, 0))] + _route_specs(tt, lambda i, e: i),
        out_specs=pl.BlockSpec((tt, d), lambda i, e: (i, 0)),
        out_shape=jax.ShapeDtypeStruct((t, d), jnp.float32),
        scratch_shapes=[pltpu.VMEM((et, tt), jnp.float32),
                        pltpu.VMEM((et, tt), jnp.bfloat16)],
        compiler_params=_params(),
        name="peer_dense",
    )(hn, u, v, *routing)


def _final_kernel(h_ref, p_ref, g_ref, o_ref):
    o_ref[...] = _rms(h_ref[...] + p_ref[...], g_ref[...])


def _final(h, p, g, tm):
    t, d = h.shape
    row = pl.BlockSpec((tm, d), lambda i: (i, 0))
    return pl.pallas_call(
        _final_kernel,
        grid=(t // tm,),
        in_specs=[row, row, pl.BlockSpec((1, d), lambda i: (0, 0))],
        out_specs=row,
        out_shape=jax.ShapeDtypeStruct((t, d), jnp.float32),
        compiler_params=_params(),
        name="final_norm",
    )(h, p, g.reshape(1, d))


def _tile(t, pref):
    while t % pref:
        pref //= 2
    return pref


def _project_in(x2, norm_g, w_in_bf):
    t = x2.shape[0]
    xn = _rmsnorm(x2, norm_g, jnp.bfloat16, _tile(t, 256))
    tm = _tile(t, 1024)
    cw = (w_in_bf.shape[1] - 3 * QKV_WIDTH)
    q = _matmul(xn, w_in_bf, 0, QKV_WIDTH, jnp.float32, tm, 1024)
    k = _matmul(xn, w_in_bf, QKV_WIDTH, QKV_WIDTH, jnp.float32, tm, 1024)
    v = _matmul(xn, w_in_bf, 2 * QKV_WIDTH, QKV_WIDTH, jnp.float32, tm, 1024)
    pc = _matmul(xn, w_in_bf, 3 * QKV_WIDTH, cw, jnp.float32, tm, 1024)
    return q, k, v, pc, xn


def _channel_mix(x2, att, cy, lw):
    t = x2.shape[0]
    cat = _catnorm(att, cy, lw["g_att"], lw["g_conv"], _tile(t, 256))
    h, hn = _outproj(cat, lw["w_out"], x2, lw["g_ffn"], _tile(t, 256))
    qp = _matmul(hn, lw["w_pq"], 0, lw["w_pq"].shape[1], jnp.bfloat16, _tile(t, 1024), 1024)
    routing = _route(qp, lw["ka"], lw["kb"], _tile(t, 256))
    peer = _peer_dense(hn, lw["u"], lw["v"], routing, _tile(t, 512), 8 * N_KEYS)
    return h, peer


def kernel(x_prompt, x_sample, cache_k_g0, cache_v_g0, cache_k_g1, cache_v_g1, cache_k_g2, cache_v_g2, state_conv, norm_mix, w_in, conv_w, norm_att_out, norm_conv_out, w_out, norm_ffn, w_pq, sub_keys_a, sub_keys_b, expert_u, expert_v, norm_final):
    bf = jnp.bfloat16
    depth = w_in.shape[0]
    b, s, d = x_prompt.shape
    bd, td, _ = x_sample.shape
    caches_k = (cache_k_g0, cache_k_g1, cache_k_g2)
    caches_v = (cache_v_g0, cache_v_g1, cache_v_g2)
    conv_c = conv_w.shape[2]

    hp = x_prompt.reshape(b * s, d)
    hs = x_sample.reshape(bd * td, d)
    nk_p = [[] for _ in range(N_GROUPS)]
    nv_p = [[] for _ in range(N_GROUPS)]
    nk_s = [[] for _ in range(N_GROUPS)]
    nv_s = [[] for _ in range(N_GROUPS)]
    nc_p, nc_s = [], []
    for l in range(depth):
        lw = dict(g_att=norm_att_out[l], g_conv=norm_conv_out[l], w_out=w_out[l].astype(bf),
                  g_ffn=norm_ffn[l], w_pq=w_pq[l].astype(bf), ka=sub_keys_a[l].astype(bf),
                  kb=sub_keys_b[l].astype(bf), u=expert_u[l].astype(bf), v=expert_v[l].astype(bf))
        w_in_bf = w_in[l].astype(bf)

        q, k, v, pc, xn = _project_in(hp, norm_mix[l], w_in_bf)
        att = _prompt_attention(q, k, v, b, s)
        cy, ns = _gated_conv(pc, jnp.zeros((b, CONV_K - 1, conv_c), jnp.float32), conv_w[l], b, s, 256)
        for g, (win, _) in enumerate(DIL_PATTERNS):
            rows = min(win, s)
            for dst, base in ((nk_p, QKV_WIDTH), (nv_p, 2 * QKV_WIDTH)):
                proj_t = _matmul_t(w_in_bf, xn, base + g * ATT_WIDTH, ATT_WIDTH, b, s, rows, 512)
                dst[g].append(jnp.transpose(proj_t.reshape(b, ATT_SLOTS, HEAD_DIM, rows), (0, 3, 1, 2)))
        nc_p.append(ns)
        h_res, peer = _channel_mix(hp, att, cy, lw)
        last = l == depth - 1
        if last:
            y_prompt = _final(h_res, peer, norm_final, _tile(b * s, 256))
        else:
            hp = h_res + peer

        q, k, v, pc, _ = _project_in(hs, norm_mix[l], w_in_bf)
        new_t = lambda a: jnp.transpose(a.reshape(bd, td, N_GROUPS, ATT_SLOTS, HEAD_DIM), (0, 2, 3, 4, 1))
        ck_t = [jnp.transpose(c[l], (0, 2, 3, 1)) for c in caches_k]
        cv_t = [jnp.transpose(c[l], (0, 2, 3, 1)) for c in caches_v]
        att_t, rolled = _sample_attention(new_t(q), new_t(k), new_t(v), ck_t, cv_t)
        att = jnp.transpose(att_t, (0, 3, 1, 2)).reshape(bd * td, ATT_WIDTH)
        for g in range(N_GROUPS):
            nk_s[g].append(jnp.transpose(rolled[2 * g], (0, 3, 1, 2)))
            nv_s[g].append(jnp.transpose(rolled[2 * g + 1], (0, 3, 1, 2)))
        cy, ns = _gated_conv(pc, state_conv[l], conv_w[l], bd, td, conv_c)
        nc_s.append(ns)
        h_res, peer = _channel_mix(hs, att, cy, lw)
        if last:
            y_sample = _final(h_res, peer, norm_final, _tile(bd * td, 256))
        else:
            hs = h_res + peer

    return (y_prompt.reshape(b, s, d), y_sample.reshape(bd, td, d),
            jnp.stack(nk_p[0], 0), jnp.stack(nv_p[0], 0),
            jnp.stack(nk_p[1], 0), jnp.stack(nv_p[1], 0),
            jnp.stack(nk_p[2], 0), jnp.stack(nv_p[2], 0),
            jnp.stack(nc_p, 0),
            jnp.stack(nk_s[0], 0), jnp.stack(nv_s[0], 0),
            jnp.stack(nk_s[1], 0), jnp.stack(nv_s[1], 0),
            jnp.stack(nk_s[2], 0), jnp.stack(nv_s[2], 0),
            jnp.stack(nc_s, 0))
```

```python
import functools
import math

import jax
import jax.numpy as jnp
from jax import lax
from jax.experimental import pallas as pl
from jax.experimental.pallas import tpu as pltpu

HEAD_DIM = 64
ATT_SLOTS = 16
DIL_PATTERNS = ((128, 1), (512, 4), (2048, 16))
N_GROUPS = len(DIL_PATTERNS)
ATT_WIDTH = ATT_SLOTS * HEAD_DIM
QKV_WIDTH = N_GROUPS * ATT_WIDTH
CONV_K = 3
N_KEYS = 128
PEER_HEADS = 8
PEER_QDIM = 256
PEER_HALF = PEER_QDIM // 2
PEER_TOPK = 16
NORM_EPS = 1e-6
NEG_INF = -1e30

LANES = 128
HEADS_PER_SLAB = LANES // HEAD_DIM
N_SLABS = ATT_WIDTH // LANES
N_BACK = 128
VMEM_LIMIT = 56 * 1024 * 1024

_NT = (((1,), (1,)), ((), ()))
_TN = (((0,), (0,)), ((), ()))


def _params(vmem=None):
    return pltpu.CompilerParams(vmem_limit_bytes=vmem or VMEM_LIMIT)


def _rms(x, g):
    return x * lax.rsqrt(jnp.mean(x * x, axis=-1, keepdims=True) + NORM_EPS) * g


def _rmsnorm_kernel(x_ref, g_ref, o_ref):
    o_ref[...] = _rms(x_ref[...], g_ref[...]).astype(o_ref.dtype)


def _rmsnorm(x, g, out_dtype, tm):
    t, d = x.shape
    return pl.pallas_call(
        _rmsnorm_kernel,
        grid=(t // tm,),
        in_specs=[pl.BlockSpec((tm, d), lambda i: (i, 0)),
                  pl.BlockSpec((1, d), lambda i: (0, 0))],
        out_specs=pl.BlockSpec((tm, d), lambda i: (i, 0)),
        out_shape=jax.ShapeDtypeStruct((t, d), out_dtype),
        compiler_params=_params(),
        name="rmsnorm",
    )(x, g.reshape(1, d))


def _mm_kernel(a_ref, b_ref, o_ref):
    o_ref[...] = jnp.dot(a_ref[...], b_ref[...],
                         preferred_element_type=jnp.float32).astype(o_ref.dtype)


def _matmul(a, b, col_start, n_cols, out_dtype, tm, tn):
    t, k = a.shape
    off = col_start // tn
    return pl.pallas_call(
        _mm_kernel,
        grid=(t // tm, n_cols // tn),
        in_specs=[pl.BlockSpec((tm, k), lambda i, j: (i, 0)),
                  pl.BlockSpec((k, tn), lambda i, j: (0, j + off))],
        out_specs=pl.BlockSpec((tm, tn), lambda i, j: (i, j)),
        out_shape=jax.ShapeDtypeStruct((t, n_cols), out_dtype),
        compiler_params=_params(),
        name="matmul",
    )(a, b)


def _mm_t_kernel(w_ref, x_ref, o_ref):
    o_ref[...] = lax.dot_general(w_ref[...], x_ref[...], (((0,), (1,)), ((), ())),
                                 preferred_element_type=jnp.float32)


def _matmul_t(w, x, col_start, n_cols, n_seq, seq, rows, tn):
    k = x.shape[1]
    per = seq // rows
    off = col_start // tn
    return pl.pallas_call(
        _mm_t_kernel,
        grid=(n_seq, n_cols // tn),
        in_specs=[pl.BlockSpec((k, tn), lambda b, j: (0, j + off)),
                  pl.BlockSpec((rows, k), lambda b, j: (b * per + per - 1, 0))],
        out_specs=pl.BlockSpec((None, tn, rows), lambda b, j: (b, j, 0)),
        out_shape=jax.ShapeDtypeStruct((n_seq, n_cols, rows), jnp.float32),
        compiler_params=_params(),
        name="matmul_t",
    )(w, x)


def _conv_kernel(gb_ref, gc_ref, h_ref, st_ref, w_ref, y_ref, ns_ref):
    u = gc_ref[...] * h_ref[...]
    s = u.shape[0]
    row = lax.broadcasted_iota(jnp.int32, u.shape, 0)
    st = st_ref[...]
    w = w_ref[...]
    u1 = jnp.where(row == 0, st[1:2, :], pltpu.roll(u, 1, 0))
    u2 = jnp.where(row == 0, st[0:1, :],
                   jnp.where(row == 1, st[1:2, :], pltpu.roll(u, 2, 0)))
    y = u2 * w[0:1, :]
    y = y + u1 * w[1:2, :]
    y = y + u * w[2:3, :]
    y_ref[...] = gb_ref[...] * y
    ns_ref[...] = u[s - 2:s, :]


def _gated_conv(pc, state, conv_w, n_seq, seq, cb):
    c = conv_w.shape[1]
    ncb = c // cb
    return pl.pallas_call(
        _conv_kernel,
        grid=(n_seq, ncb),
        in_specs=[pl.BlockSpec((seq, cb), lambda b, j: (b, j)),
                  pl.BlockSpec((seq, cb), lambda b, j: (b, ncb + j)),
                  pl.BlockSpec((seq, cb), lambda b, j: (b, 2 * ncb + j)),
                  pl.BlockSpec((None, CONV_K - 1, cb), lambda b, j: (b, 0, j)),
                  pl.BlockSpec((CONV_K, cb), lambda b, j: (0, j))],
        out_specs=[pl.BlockSpec((seq, cb), lambda b, j: (b, j)),
                   pl.BlockSpec((None, CONV_K - 1, cb), lambda b, j: (b, 0, j))],
        out_shape=[jax.ShapeDtypeStruct((n_seq * seq, c), jnp.float32),
                   jax.ShapeDtypeStruct((n_seq, CONV_K - 1, c), jnp.float32)],
        compiler_params=_params(),
        name="gated_conv",
    )(pc, pc, pc, state, conv_w)


def _slope_like(shape, group, head):
    i = (group * ATT_SLOTS + 1 + head).astype(jnp.float32)
    n = float(N_GROUPS * ATT_SLOTS)
    return jnp.exp2(jnp.broadcast_to(-8.0 * i / n, shape))


def _merge_stats(stats):
    ms = [s[0] for s in stats]
    m = functools.reduce(jnp.maximum, ms)
    den = num = None
    for (mg, lg, ng) in stats:
        a = jnp.exp(mg - m)
        den = a * lg if den is None else den + a * lg
        num = a * ng if num is None else num + a * ng
    return num / den


def _prompt_attn_kernel(q0, k0, v0, q1, k1, v1, q2, k2, v2, o_ref, *stat_refs, seq):
    slab = pl.program_id(1)
    lane = lax.broadcasted_iota(jnp.int32, (1, LANES), 1)
    first = lane < HEAD_DIM
    scale = HEAD_DIM ** -0.5

    refs = ((q0, k0, v0), (q1, k1, v1), (q2, k2, v2))
    for g, (win, dil) in enumerate(DIL_PATTERNS):
        q_ref, k_ref, v_ref = refs[g]
        m_ref, l_ref, n_ref = stat_refs[3 * g:3 * g + 3]
        span = dil * N_BACK
        nb = seq // span
        two = nb > 1
        nk = (2 if two else 1) * N_BACK
        qi = lax.broadcasted_iota(jnp.int32, (N_BACK, nk), 0)
        kc = lax.broadcasted_iota(jnp.int32, (N_BACK, nk), 1)
        steps = (N_BACK if two else 0) + qi - kc
        in_band = (steps >= 0) & (steps <= N_BACK)
        dist = (steps * dil).astype(jnp.float32)
        slopes = [_slope_like((1, nk), g, slab * HEADS_PER_SLAB + h)
                  for h in range(HEADS_PER_SLAB)]

        def tile(t, carry, q_ref=q_ref, k_ref=k_ref, v_ref=v_ref, m_ref=m_ref, l_ref=l_ref,
                 n_ref=n_ref, dil=dil, span=span, nb=nb, two=two, kc=kc, in_band=in_band,
                 dist=dist, slopes=slopes):
            r = t // nb
            n = t - r * nb
            start = n * span + r
            rows = pl.ds(start, N_BACK, stride=dil)
            q = q_ref[rows, :] * scale
            if two:
                prev = jnp.maximum(start - span, r)
                prows = pl.ds(prev, N_BACK, stride=dil)
                kk = jnp.concatenate([k_ref[prows, :], k_ref[rows, :]], axis=0)
                vv = jnp.concatenate([v_ref[prows, :], v_ref[rows, :]], axis=0)
                valid = in_band & ((kc >= N_BACK) | (n > 0))
            else:
                kk, vv, valid = k_ref[rows, :], v_ref[rows, :], in_band
            kk = kk.astype(jnp.bfloat16)
            vv = vv.astype(jnp.bfloat16)
            stats = []
            for h in range(HEADS_PER_SLAB):
                mine = first if h == 0 else jnp.logical_not(first)
                qm = jnp.where(mine, q, 0.0).astype(jnp.bfloat16)
                s = lax.dot_general(qm, kk, _NT, preferred_element_type=jnp.float32)
                s = s + jnp.where(valid, -slopes[h] * dist, NEG_INF)
                m = jnp.max(s, axis=-1, keepdims=True)
                p = jnp.exp(s - m)
                l = jnp.sum(p, axis=-1, keepdims=True)
                o = jnp.dot(p.astype(jnp.bfloat16), vv, preferred_element_type=jnp.float32)
                stats.append((m, l, o))
            m_ref[rows, :] = jnp.where(first, stats[0][0], stats[1][0])
            l_ref[rows, :] = jnp.where(first, stats[0][1], stats[1][1])
            n_ref[rows, :] = jnp.where(first, stats[0][2], stats[1][2])
            return carry

        lax.fori_loop(0, dil * nb, tile, 0, unroll=4)

    chunk = 256

    def merge(c, carry):
        rows = pl.ds(pl.multiple_of(c * chunk, chunk), chunk)
        o_ref[rows, :] = _merge_stats([[stat_refs[3 * g + j][rows, :] for j in range(3)]
                                       for g in range(N_GROUPS)])
        return carry

    lax.fori_loop(0, seq // chunk, merge, 0)


def _prompt_attention(q, k, v, n_seq, seq):
    assert seq % (DIL_PATTERNS[-1][1] * N_BACK) == 0
    in_specs = []
    for g in range(N_GROUPS):
        for _ in range(3):
            in_specs.append(pl.BlockSpec((seq, LANES), lambda b, j, g=g: (b, g * N_SLABS + j)))
    args = []
    for g in range(N_GROUPS):
        args += [q, k, v]
    return pl.pallas_call(
        functools.partial(_prompt_attn_kernel, seq=seq),
        grid=(n_seq, N_SLABS),
        in_specs=in_specs,
        out_specs=pl.BlockSpec((seq, LANES), lambda b, j: (b, j)),
        out_shape=jax.ShapeDtypeStruct((n_seq * seq, ATT_WIDTH), jnp.float32),
        scratch_shapes=[pltpu.VMEM((seq, LANES), jnp.float32)] * (3 * N_GROUPS),
        compiler_params=_params(),
        name="prompt_attention",
    )(*args)


SAMPLE_HEADS_PER_STEP = 4


def _sample_kernel(*refs, t_new):
    qn_ref, kn_ref, vn_ref = refs[:3]
    cache_refs = refs[3:3 + 2 * N_GROUPS]
    out_refs = refs[3 + 2 * N_GROUPS:3 + 4 * N_GROUPS]
    o_ref = refs[3 + 4 * N_GROUPS]
    hb = o_ref.shape[0]
    scale = HEAD_DIM ** -0.5
    lane = lax.broadcasted_iota(jnp.int32, (1, LANES), 1)
    trow = lax.broadcasted_iota(jnp.int32, (t_new, 1), 0)

    def widen(x):
        return jnp.concatenate([x, jnp.zeros((x.shape[0], LANES - t_new), x.dtype)], axis=1)

    def per_head(h, carry):
        head = pl.program_id(1) * hb + h
        score_tiles = []
        for g, (win, dil) in enumerate(DIL_PATTERNS):
            ck = cache_refs[2 * g]
            n_tiles = win // LANES
            per_pass = min(dil, t_new)
            q_t = qn_ref[g, h] * scale
            k_new = widen(kn_ref[g, h])
            slope = _slope_like((1, LANES), g, head)
            res = lane & (dil - 1)
            tiles = [jnp.zeros((t_new, LANES), jnp.float32) for _ in range(n_tiles + 1)]
            for k in range(t_new // per_pass):
                q_mix = None
                for t in range(k * per_pass, (k + 1) * per_pass):
                    q_col = jnp.broadcast_to(q_t[:, t:t + 1], (HEAD_DIM, LANES))
                    row = jnp.sum(k_new * q_col, axis=0, keepdims=True)
                    tiles[n_tiles] = jnp.where(trow == t, row, tiles[n_tiles])
                    term = q_col if per_pass == 1 else jnp.where(res == t % dil, q_col, 0.0)
                    q_mix = term if q_mix is None else q_mix + term
                in_pass = (trow >= k * per_pass) & (trow < (k + 1) * per_pass)
                for j in range(n_tiles):
                    row = jnp.sum(ck[h, :, j * LANES:(j + 1) * LANES] * q_mix, axis=0, keepdims=True)
                    tiles[j] = jnp.where(in_pass, row, tiles[j])
            for j in range(n_tiles + 1):
                if j < n_tiles:
                    dist = win + trow - (j * LANES + lane)
                    valid = ((dist & (dil - 1)) == 0) & (dist <= win)
                else:
                    dist = trow - lane
                    valid = ((dist & (dil - 1)) == 0) & (dist >= 0)
                tiles[j] = jnp.where(valid, tiles[j] - slope * dist.astype(jnp.float32), NEG_INF)
            score_tiles.append(tiles)
        flat = [s for tiles in score_tiles for s in tiles]
        m = functools.reduce(jnp.maximum, [jnp.max(s, axis=1, keepdims=True) for s in flat])
        denom = functools.reduce(lambda a, b: a + b,
                                 [jnp.sum(jnp.exp(s - m), axis=1, keepdims=True) for s in flat])
        probs = [[jnp.exp(s - m) / denom for s in tiles] for tiles in score_tiles]
        pass_sums = {}
        for g, (win, dil) in enumerate(DIL_PATTERNS):
            cv = cache_refs[2 * g + 1]
            per_pass = min(dil, t_new)
            for k in range(t_new // per_pass):
                in_pass = (trow >= k * per_pass) & (trow < (k + 1) * per_pass)
                acc = None
                for j in range(win // LANES):
                    w_row = jnp.sum(jnp.where(in_pass, probs[g][j], 0.0), axis=0, keepdims=True)
                    term = cv[h, :, j * LANES:(j + 1) * LANES] * w_row
                    acc = term if acc is None else acc + term
                pass_sums[g, k] = acc
        out = jnp.zeros((HEAD_DIM, LANES), jnp.float32)
        for t in range(t_new):
            tot = None
            for g, (win, dil) in enumerate(DIL_PATTERNS):
                per_pass = min(dil, t_new)
                mine = pass_sums[g, t // per_pass]
                if per_pass > 1:
                    mine = jnp.where((lane & (dil - 1)) == t % dil, mine, 0.0)
                mine = mine + widen(vn_ref[g, h]) * probs[g][win // LANES][t:t + 1, :]
                tot = mine if tot is None else tot + mine
            out = jnp.where(lane == t, jnp.sum(tot, axis=1, keepdims=True), out)
        o_ref[h] = out[:, 0:t_new]
        for g, (win, dil) in enumerate(DIL_PATTERNS):
            for src, new, dst in ((cache_refs[2 * g], kn_ref, out_refs[2 * g]),
                                  (cache_refs[2 * g + 1], vn_ref, out_refs[2 * g + 1])):
                rolled = pltpu.roll(src[h], win - t_new, 1)
                top = pltpu.roll(widen(new[g, h]), LANES - t_new, 1)
                if win > LANES:
                    dst[h, :, 0:win - LANES] = rolled[:, 0:win - LANES]
                dst[h, :, win - LANES:win] = jnp.where(lane >= LANES - t_new, top,
                                                       rolled[:, win - LANES:win])
        return carry

    lax.fori_loop(0, hb, per_head, 0)


def _sample_attention(q_new, k_new, v_new, caches_k, caches_v):
    n, _, nh, dh, t_new = q_new.shape
    hb = SAMPLE_HEADS_PER_STEP
    new_spec = pl.BlockSpec((None, N_GROUPS, hb, dh, t_new), lambda b, j: (b, 0, j, 0, 0))
    in_specs, args, out_specs, out_shape = [new_spec] * 3, [q_new, k_new, v_new], [], []
    for g, (win, dil) in enumerate(DIL_PATTERNS):
        assert caches_k[g].shape == (n, nh, dh, win) and win % LANES == 0 and dil & (dil - 1) == 0
        spec = pl.BlockSpec((None, hb, dh, win), lambda b, j: (b, j, 0, 0))
        in_specs += [spec, spec]
        args += [caches_k[g], caches_v[g]]
        out_specs += [spec, spec]
        out_shape += [jax.ShapeDtypeStruct(caches_k[g].shape, jnp.float32)] * 2
    out_specs.append(pl.BlockSpec((None, hb, dh, t_new), lambda b, j: (b, j, 0, 0)))
    out_shape.append(jax.ShapeDtypeStruct((n, nh, dh, t_new), jnp.float32))
    res = pl.pallas_call(
        functools.partial(_sample_kernel, t_new=t_new),
        grid=(n, nh // hb),
        in_specs=in_specs,
        out_specs=out_specs,
        out_shape=out_shape,
        compiler_params=_params(),
        name="sample_attention",
    )(*args)
    return res[-1], res[:-1]


def _catnorm_kernel(att_ref, cy_ref, ga_ref, gc_ref, o_ref):
    wa = att_ref.shape[1]
    o_ref[:, 0:wa] = _rms(att_ref[...], ga_ref[...]).astype(o_ref.dtype)
    o_ref[:, wa:] = _rms(cy_ref[...], gc_ref[...]).astype(o_ref.dtype)


def _catnorm(att, cy, g_att, g_conv, tm):
    t, wa = att.shape
    wc = cy.shape[1]
    return pl.pallas_call(
        _catnorm_kernel,
        grid=(t // tm,),
        in_specs=[pl.BlockSpec((tm, wa), lambda i: (i, 0)),
                  pl.BlockSpec((tm, wc), lambda i: (i, 0)),
                  pl.BlockSpec((1, wa), lambda i: (0, 0)),
                  pl.BlockSpec((1, wc), lambda i: (0, 0))],
        out_specs=pl.BlockSpec((tm, wa + wc), lambda i: (i, 0)),
        out_shape=jax.ShapeDtypeStruct((t, wa + wc), jnp.bfloat16),
        compiler_params=_params(),
        name="catnorm",
    )(att, cy, g_att.reshape(1, wa), g_conv.reshape(1, wc))


def _outproj_kernel(a_ref, w_ref, x_ref, g_ref, h_ref, hn_ref):
    h = x_ref[...] + jnp.dot(a_ref[...], w_ref[...], preferred_element_type=jnp.float32)
    h_ref[...] = h
    hn_ref[...] = _rms(h, g_ref[...]).astype(hn_ref.dtype)


def _outproj(cat, w_out, x, g_ffn, tm):
    t, k = cat.shape
    d = w_out.shape[1]
    return pl.pallas_call(
        _outproj_kernel,
        grid=(t // tm,),
        in_specs=[pl.BlockSpec((tm, k), lambda i: (i, 0)),
                  pl.BlockSpec((k, d), lambda i: (0, 0)),
                  pl.BlockSpec((tm, d), lambda i: (i, 0)),
                  pl.BlockSpec((1, d), lambda i: (0, 0))],
        out_specs=[pl.BlockSpec((tm, d), lambda i: (i, 0)),
                   pl.BlockSpec((tm, d), lambda i: (i, 0))],
        out_shape=[jax.ShapeDtypeStruct((t, d), jnp.float32),
                   jax.ShapeDtypeStruct((t, d), jnp.bfloat16)],
        compiler_params=_params(),
        name="outproj",
    )(cat, w_out, x, g_ffn.reshape(1, d))


_N_RANK = PEER_TOPK + 1
_CAND_PER_RANK = tuple(_N_RANK // (i + 1) for i in range(_N_RANK))
_N_CAND = sum(_CAND_PER_RANK)
_CAND_ROWS = -(-_N_CAND // 8) * 8


def _top_values(xs, k):
    tt = xs[0].shape[1]
    slot = lax.broadcasted_iota(jnp.int32, (k, tt), 0)
    rowfs = [lax.broadcasted_iota(jnp.int32, x.shape, 0).astype(jnp.float32) for x in xs]

    def step(r, carry):
        new = []
        for (x, vals), rowf in zip(carry, rowfs):
            m = jnp.max(x, axis=0, keepdims=True)
            first = jnp.min(jnp.where(x == m, rowf, float(x.shape[0])), axis=0, keepdims=True)
            new.append((jnp.where(rowf == first, -jnp.inf, x), jnp.where(slot == r, m, vals)))
        return tuple(new)

    init = tuple((x, jnp.zeros((k, tt), jnp.float32)) for x in xs)
    return [vals for _, vals in lax.fori_loop(0, k, step, init)]


def _route_kernel(q_ref, ka_ref, kb_ref, thr_ref, sb_ref, ea_ref, eb_ref):
    ka = ka_ref[...]
    kb = kb_ref[...]
    tt = q_ref.shape[0]
    sub = N_KEYS // 8

    def per_head(h, carry):
        c0 = pl.multiple_of(h * PEER_QDIM, PEER_QDIM)
        qa = q_ref[:, pl.ds(c0, PEER_HALF)]
        qb = q_ref[:, pl.ds(c0 + PEER_HALF, PEER_HALF)]
        sa = lax.dot_general(ka, qa, _NT, preferred_element_type=jnp.float32)
        sb = lax.dot_general(kb, qb, _NT, preferred_element_type=jnp.float32)
        va, vb = _top_values((sa, sb), _N_RANK)
        pieces = [va[i:i + 1, :] + vb[0:n, :] for i, n in enumerate(_CAND_PER_RANK)]
        pieces.append(jnp.full((_CAND_ROWS - _N_CAND, tt), -jnp.inf, jnp.float32))
        (top,) = _top_values((jnp.concatenate(pieces, axis=0),), _N_RANK)
        z = jnp.sum(jnp.exp(top[0:PEER_TOPK, :] - top[0:1, :]), axis=0, keepdims=True)
        tau = 0.5 * (top[PEER_TOPK - 1:PEER_TOPK, :] + top[PEER_TOPK:PEER_TOPK + 1, :])
        thr = tau - sa
        ea = jnp.exp(sa - va[0:1, :])
        eb = jnp.exp(sb - vb[0:1, :]) / z
        for lt in range(tt // LANES):
            ls = slice(lt * LANES, (lt + 1) * LANES)
            thr_ref[h, lt] = thr[:, ls].reshape(sub, 8, LANES)
            ea_ref[h, lt] = ea[:, ls].reshape(sub, 8, LANES)
            sb_ref[h, lt] = sb[:, ls]
            eb_ref[h, lt] = eb[:, ls]
        return carry

    lax.fori_loop(0, PEER_HEADS, per_head, 0)


def _route_specs(tt, imap):
    n_lt = tt // LANES
    row = pl.BlockSpec((PEER_HEADS, n_lt, N_KEYS // 8, 8, LANES), lambda *g: (0, imap(*g), 0, 0, 0))
    full = pl.BlockSpec((PEER_HEADS, n_lt, N_KEYS, LANES), lambda *g: (0, imap(*g), 0, 0))
    return [row, full, row, full]


def _route(qp, ka, kb, tt):
    t = qp.shape[0]
    row_shape = jax.ShapeDtypeStruct((PEER_HEADS, t // LANES, N_KEYS // 8, 8, LANES), jnp.float32)
    full_shape = jax.ShapeDtypeStruct((PEER_HEADS, t // LANES, N_KEYS, LANES), jnp.float32)
    return pl.pallas_call(
        _route_kernel,
        grid=(t // tt,),
        in_specs=[pl.BlockSpec((tt, PEER_HEADS * PEER_QDIM), lambda i: (i, 0)),
                  pl.BlockSpec((N_KEYS, PEER_HALF), lambda i: (0, 0)),
                  pl.BlockSpec((N_KEYS, PEER_HALF), lambda i: (0, 0))],
        out_specs=_route_specs(tt, lambda i: i),
        out_shape=[row_shape, full_shape, row_shape, full_shape],
        compiler_params=_params(),
        name="peer_route",
    )(qp, ka, kb)


def _gelu(x):
    return 0.5 * x * (1.0 + lax.erf(x * math.sqrt(0.5)))


def _peer_kernel(hn_ref, u_ref, v_ref, thr_ref, sb_ref, ea_ref, eb_ref, o_ref, h_scr, w_scr):
    e = pl.program_id(1)
    et, tt = h_scr.shape
    assert et == 8 * N_KEYS

    @pl.when(e == 0)
    def _():
        o_ref[...] = jnp.zeros(o_ref.shape, o_ref.dtype)

    h_scr[...] = lax.dot_general(u_ref[...], hn_ref[...], _NT,
                                 preferred_element_type=jnp.float32)

    def per_lane_tile(lt, carry):
        ls = pl.ds(pl.multiple_of(lt * LANES, LANES), LANES)
        for a in range(et // N_KEYS):
            rows = pl.ds(a * N_KEYS, N_KEYS)
            acc = None
            for h in range(PEER_HEADS):
                keep = sb_ref[h, lt] >= thr_ref[h, lt, e, a:a + 1, :]
                gate = jnp.where(keep, eb_ref[h, lt], 0.0) * ea_ref[h, lt, e, a:a + 1, :]
                acc = gate if acc is None else acc + gate
            w_scr[rows, ls] = (acc * _gelu(h_scr[rows, ls])).astype(w_scr.dtype)
        return carry

    lax.fori_loop(0, tt // LANES, per_lane_tile, 0)
    o_ref[...] += lax.dot_general(w_scr[...], v_ref[...], _TN,
                                  preferred_element_type=jnp.float32)


def _peer_dense(hn, u, v, routing, tt, et):
    t, d = hn.shape
    n_exp = u.shape[0]
    return pl.pallas_call(
        _peer_kernel,
        grid=(t // tt, n_exp // et),
        in_specs=[pl.BlockSpec((tt, d), lambda i, e: (i, 0)),
                  pl.BlockSpec((et, d), lambda i, e: (e, 0)),
                  pl.BlockSpec((et, d), lambda i, e: (e, 0))] + _route_specs(tt, lambda i, e: i),
        out_specs=pl.BlockSpec((tt, d), lambda i, e: (i, 0)),
        out_shape=jax.ShapeDtypeStruct((t, d), jnp.float32),
        scratch_shapes=[pltpu.VMEM((et, tt), jnp.float32),
                        pltpu.VMEM((et, tt), jnp.bfloat16)],
        compiler_params=_params(),
        name="peer_dense",
    )(hn, u, v, *routing)


def _final_kernel(h_ref, p_ref, g_ref, o_ref):
    o_ref[...] = _rms(h_ref[...] + p_ref[...], g_ref[...])


def _final(h, p, g, tm):
    t, d = h.shape
    row = pl.BlockSpec((tm, d), lambda i: (i, 0))
    return pl.pallas_call(
        _final_kernel,
        grid=(t // tm,),
        in_specs=[row, row, pl.BlockSpec((1, d), lambda i: (0, 0))],
        out_specs=row,
        out_shape=jax.ShapeDtypeStruct((t, d), jnp.float32),
        compiler_params=_params(),
        name="final_norm",
    )(h, p, g.reshape(1, d))


def _tile(t, pref):
    while t % pref:
        pref //= 2
    return pref


def _project_in(x2, norm_g, w_in_bf):
    t = x2.shape[0]
    xn = _rmsnorm(x2, norm_g, jnp.bfloat16, _tile(t, 256))
    tm = _tile(t, 1024)
    cw = (w_in_bf.shape[1] - 3 * QKV_WIDTH)
    q = _matmul(xn, w_in_bf, 0, QKV_WIDTH, jnp.float32, tm, 1024)
    k = _matmul(xn, w_in_bf, QKV_WIDTH, QKV_WIDTH, jnp.float32, tm, 1024)
    v = _matmul(xn, w_in_bf, 2 * QKV_WIDTH, QKV_WIDTH, jnp.float32, tm, 1024)
    pc = _matmul(xn, w_in_bf, 3 * QKV_WIDTH, cw, jnp.float32, tm, 1024)
    return q, k, v, pc, xn


def _channel_mix(x2, att, cy, lw):
    t = x2.shape[0]
    cat = _catnorm(att, cy, lw["g_att"], lw["g_conv"], _tile(t, 256))
    h, hn = _outproj(cat, lw["w_out"], x2, lw["g_ffn"], _tile(t, 256))
    qp = _matmul(hn, lw["w_pq"], 0, lw["w_pq"].shape[1], jnp.bfloat16, _tile(t, 1024), 1024)
    routing = _route(qp, lw["ka"], lw["kb"], _tile(t, 512))
    peer = _peer_dense(hn, lw["u"], lw["v"], routing, _tile(t, 512), 8 * N_KEYS)
    return h, peer


def kernel(x_prompt, x_sample, cache_k_g0, cache_v_g0, cache_k_g1, cache_v_g1, cache_k_g2, cache_v_g2, state_conv, norm_mix, w_in, conv_w, norm_att_out, norm_conv_out, w_out, norm_ffn, w_pq, sub_keys_a, sub_keys_b, expert_u, expert_v, norm_final):
    bf = jnp.bfloat16
    depth = w_in.shape[0]
    b, s, d = x_prompt.shape
    bd, td, _ = x_sample.shape
    caches_k = (cache_k_g0, cache_k_g1, cache_k_g2)
    caches_v = (cache_v_g0, cache_v_g1, cache_v_g2)
    conv_c = conv_w.shape[2]

    hp = x_prompt.reshape(b * s, d)
    hs = x_sample.reshape(bd * td, d)
    nk_p = [[] for _ in range(N_GROUPS)]
    nv_p = [[] for _ in range(N_GROUPS)]
    nk_s = [[] for _ in range(N_GROUPS)]
    nv_s = [[] for _ in range(N_GROUPS)]
    nc_p, nc_s = [], []
    for l in range(depth):
        lw = dict(g_att=norm_att_out[l], g_conv=norm_conv_out[l], w_out=w_out[l].astype(bf),
                  g_ffn=norm_ffn[l], w_pq=w_pq[l].astype(bf), ka=sub_keys_a[l].astype(bf),
                  kb=sub_keys_b[l].astype(bf), u=expert_u[l].astype(bf), v=expert_v[l].astype(bf))
        w_in_bf = w_in[l].astype(bf)

        q, k, v, pc, xn = _project_in(hp, norm_mix[l], w_in_bf)
        att = _prompt_attention(q, k, v, b, s)
        cy, ns = _gated_conv(pc, jnp.zeros((b, CONV_K - 1, conv_c), jnp.float32), conv_w[l], b, s, 256)
        for g, (win, _) in enumerate(DIL_PATTERNS):
            rows = min(win, s)
            for dst, base in ((nk_p, QKV_WIDTH), (nv_p, 2 * QKV_WIDTH)):
                proj_t = _matmul_t(w_in_bf, xn, base + g * ATT_WIDTH, ATT_WIDTH, b, s, rows, 512)
                dst[g].append(jnp.transpose(proj_t.reshape(b, ATT_SLOTS, HEAD_DIM, rows), (0, 3, 1, 2)))
        nc_p.append(ns)
        h_res, peer = _channel_mix(hp, att, cy, lw)
        last = l == depth - 1
        if last:
            y_prompt = _final(h_res, peer, norm_final, _tile(b * s, 256))
        else:
            hp = h_res + peer

        q, k, v, pc, _ = _project_in(hs, norm_mix[l], w_in_bf)
        new_t = lambda a: jnp.transpose(a.reshape(bd, td, N_GROUPS, ATT_SLOTS, HEAD_DIM), (0, 2, 3, 4, 1))
        ck_t = [jnp.transpose(c[l], (0, 2, 3, 1)) for c in caches_k]
        cv_t = [jnp.transpose(c[l], (0, 2, 3, 1)) for c in caches_v]
        att_t, rolled = _sample_attention(new_t(q), new_t(k), new_t(v), ck_t, cv_t)
        att = jnp.transpose(att_t, (0, 3, 1, 2)).reshape(bd * td, ATT_WIDTH)
        for g in range(N_GROUPS):
            nk_s[g].append(jnp.transpose(rolled[2 * g], (0, 3, 1, 2)))
            nv_s[g].append(jnp.transpose(rolled[2 * g + 1], (0, 3, 1, 2)))
        cy, ns = _gated_conv(pc, state_conv[l], conv_w[l], bd, td, conv_c)
        nc_s.append(ns)
        h_res, peer = _channel_mix(hs, att, cy, lw)
        if last:
            y_sample = _final(h_res, peer, norm_final, _tile(bd * td, 256))
        else:
            hs = h_res + peer

    return (y_prompt.reshape(b, s, d), y_sample.reshape(bd, td, d),
            jnp.stack(nk_p[0], 0), jnp.stack(nv_p[0], 0),
            jnp.stack(nk_p[1], 0), jnp.stack(nv_p[1], 0),
            jnp.stack(nk_p[2], 0), jnp.stack(nv_p[2], 0),
            jnp.stack(nc_p, 0),
            jnp.stack(nk_s[0], 0), jnp.stack(nv_s[0], 0),
            jnp.stack(nk_s[1], 0), jnp.stack(nv_s[1], 0),
            jnp.stack(nk_s[2], 0), jnp.stack(nv_s[2], 0),
            jnp.stack(nc_s, 0))
```

```python
import functools
import math

import jax
import jax.numpy as jnp
from jax import lax
from jax.experimental import pallas as pl
from jax.experimental.pallas import tpu as pltpu

HEAD_DIM = 64
ATT_SLOTS = 16
DIL_PATTERNS = ((128, 1), (512, 4), (2048, 16))
N_GROUPS = len(DIL_PATTERNS)
ATT_WIDTH = ATT_SLOTS * HEAD_DIM
QKV_WIDTH = N_GROUPS * ATT_WIDTH
CONV_K = 3
N_KEYS = 128
PEER_HEADS = 8
PEER_QDIM = 256
PEER_HALF = PEER_QDIM // 2
PEER_TOPK = 16
NORM_EPS = 1e-6
NEG_INF = -1e30

LANES = 128
HEADS_PER_SLAB = LANES // HEAD_DIM
N_SLABS = ATT_WIDTH // LANES
N_BACK = 128
VMEM_LIMIT = 56 * 1024 * 1024

_NT = (((1,), (1,)), ((), ()))
_TN = (((0,), (0,)), ((), ()))


def _params(vmem=None):
    return pltpu.CompilerParams(vmem_limit_bytes=vmem or VMEM_LIMIT)


def _rms(x, g):
    return x * lax.rsqrt(jnp.mean(x * x, axis=-1, keepdims=True) + NORM_EPS) * g


def _rmsnorm_kernel(x_ref, g_ref, o_ref):
    o_ref[...] = _rms(x_ref[...], g_ref[...]).astype(o_ref.dtype)


def _rmsnorm(x, g, out_dtype, tm):
    t, d = x.shape
    return pl.pallas_call(
        _rmsnorm_kernel,
        grid=(t // tm,),
        in_specs=[pl.BlockSpec((tm, d), lambda i: (i, 0)),
                  pl.BlockSpec((1, d), lambda i: (0, 0))],
        out_specs=pl.BlockSpec((tm, d), lambda i: (i, 0)),
        out_shape=jax.ShapeDtypeStruct((t, d), out_dtype),
        compiler_params=_params(),
        name="rmsnorm",
    )(x, g.reshape(1, d))


def _mm_kernel(a_ref, b_ref, o_ref):
    o_ref[...] = jnp.dot(a_ref[...], b_ref[...],
                         preferred_element_type=jnp.float32).astype(o_ref.dtype)


def _matmul(a, b, col_start, n_cols, out_dtype, tm, tn):
    t, k = a.shape
    off = col_start // tn
    return pl.pallas_call(
        _mm_kernel,
        grid=(t // tm, n_cols // tn),
        in_specs=[pl.BlockSpec((tm, k), lambda i, j: (i, 0)),
                  pl.BlockSpec((k, tn), lambda i, j: (0, j + off))],
        out_specs=pl.BlockSpec((tm, tn), lambda i, j: (i, j)),
        out_shape=jax.ShapeDtypeStruct((t, n_cols), out_dtype),
        compiler_params=_params(),
        name="matmul",
    )(a, b)


def _mm_t_kernel(w_ref, x_ref, o_ref):
    o_ref[...] = lax.dot_general(w_ref[...], x_ref[...], (((0,), (1,)), ((), ())),
                                 preferred_element_type=jnp.float32)


def _matmul_t(w, x, col_start, n_cols, n_seq, seq, rows, tn):
    k = x.shape[1]
    per = seq // rows
    off = col_start // tn
    return pl.pallas_call(
        _mm_t_kernel,
        grid=(n_seq, n_cols // tn),
        in_specs=[pl.BlockSpec((k, tn), lambda b, j: (0, j + off)),
                  pl.BlockSpec((rows, k), lambda b, j: (b * per + per - 1, 0))],
        out_specs=pl.BlockSpec((None, tn, rows), lambda b, j: (b, j, 0)),
        out_shape=jax.ShapeDtypeStruct((n_seq, n_cols, rows), jnp.float32),
        compiler_params=_params(),
        name="matmul_t",
    )(w, x)


def _conv_kernel(gb_ref, gc_ref, h_ref, st_ref, w_ref, y_ref, ns_ref):
    u = gc_ref[...] * h_ref[...]
    s = u.shape[0]
    row = lax.broadcasted_iota(jnp.int32, u.shape, 0)
    st = st_ref[...]
    w = w_ref[...]
    u1 = jnp.where(row == 0, st[1:2, :], pltpu.roll(u, 1, 0))
    u2 = jnp.where(row == 0, st[0:1, :],
                   jnp.where(row == 1, st[1:2, :], pltpu.roll(u, 2, 0)))
    y = u2 * w[0:1, :]
    y = y + u1 * w[1:2, :]
    y = y + u * w[2:3, :]
    y_ref[...] = gb_ref[...] * y
    ns_ref[...] = u[s - 2:s, :]


def _gated_conv(pc, state, conv_w, n_seq, seq, cb):
    c = conv_w.shape[1]
    ncb = c // cb
    return pl.pallas_call(
        _conv_kernel,
        grid=(n_seq, ncb),
        in_specs=[pl.BlockSpec((seq, cb), lambda b, j: (b, j)),
                  pl.BlockSpec((seq, cb), lambda b, j: (b, ncb + j)),
                  pl.BlockSpec((seq, cb), lambda b, j: (b, 2 * ncb + j)),
                  pl.BlockSpec((None, CONV_K - 1, cb), lambda b, j: (b, 0, j)),
                  pl.BlockSpec((CONV_K, cb), lambda b, j: (0, j))],
        out_specs=[pl.BlockSpec((seq, cb), lambda b, j: (b, j)),
                   pl.BlockSpec((None, CONV_K - 1, cb), lambda b, j: (b, 0, j))],
        out_shape=[jax.ShapeDtypeStruct((n_seq * seq, c), jnp.float32),
                   jax.ShapeDtypeStruct((n_seq, CONV_K - 1, c), jnp.float32)],
        compiler_params=_params(),
        name="gated_conv",
    )(pc, pc, pc, state, conv_w)


def _slope_like(shape, group, head):
    i = (group * ATT_SLOTS + 1 + head).astype(jnp.float32)
    n = float(N_GROUPS * ATT_SLOTS)
    return jnp.exp2(jnp.broadcast_to(-8.0 * i / n, shape))


def _merge_stats(stats):
    ms = [s[0] for s in stats]
    m = functools.reduce(jnp.maximum, ms)
    den = num = None
    for (mg, lg, ng) in stats:
        a = jnp.exp(mg - m)
        den = a * lg if den is None else den + a * lg
        num = a * ng if num is None else num + a * ng
    return num / den


def _prompt_attn_kernel(q0, k0, v0, q1, k1, v1, q2, k2, v2, o_ref, *stat_refs, seq):
    slab = pl.program_id(1)
    lane = lax.broadcasted_iota(jnp.int32, (1, LANES), 1)
    first = lane < HEAD_DIM
    scale = HEAD_DIM ** -0.5

    refs = ((q0, k0, v0), (q1, k1, v1), (q2, k2, v2))
    for g, (win, dil) in enumerate(DIL_PATTERNS):
        q_ref, k_ref, v_ref = refs[g]
        m_ref, l_ref, n_ref = stat_refs[3 * g:3 * g + 3]
        span = dil * N_BACK
        nb = seq // span
        two = nb > 1
        nk = (2 if two else 1) * N_BACK
        qi = lax.broadcasted_iota(jnp.int32, (N_BACK, nk), 0)
        kc = lax.broadcasted_iota(jnp.int32, (N_BACK, nk), 1)
        steps = (N_BACK if two else 0) + qi - kc
        in_band = (steps >= 0) & (steps <= N_BACK)
        dist = (steps * dil).astype(jnp.float32)
        slopes = [_slope_like((1, nk), g, slab * HEADS_PER_SLAB + h)
                  for h in range(HEADS_PER_SLAB)]

        def tile(t, carry, q_ref=q_ref, k_ref=k_ref, v_ref=v_ref, m_ref=m_ref, l_ref=l_ref,
                 n_ref=n_ref, dil=dil, span=span, nb=nb, two=two, kc=kc, in_band=in_band,
                 dist=dist, slopes=slopes):
            r = t // nb
            n = t - r * nb
            start = n * span + r
            rows = pl.ds(start, N_BACK, stride=dil)
            q = q_ref[rows, :] * scale
            if two:
                prev = jnp.maximum(start - span, r)
                prows = pl.ds(prev, N_BACK, stride=dil)
                kk = jnp.concatenate([k_ref[prows, :], k_ref[rows, :]], axis=0)
                vv = jnp.concatenate([v_ref[prows, :], v_ref[rows, :]], axis=0)
                valid = in_band & ((kc >= N_BACK) | (n > 0))
            else:
                kk, vv, valid = k_ref[rows, :], v_ref[rows, :], in_band
            kk = kk.astype(jnp.bfloat16)
            vv = vv.astype(jnp.bfloat16)
            stats = []
            for h in range(HEADS_PER_SLAB):
                mine = first if h == 0 else jnp.logical_not(first)
                qm = jnp.where(mine, q, 0.0).astype(jnp.bfloat16)
                s = lax.dot_general(qm, kk, _NT, preferred_element_type=jnp.float32)
                s = s + jnp.where(valid, -slopes[h] * dist, NEG_INF)
                m = jnp.max(s, axis=-1, keepdims=True)
                p = jnp.exp(s - m)
                l = jnp.sum(p, axis=-1, keepdims=True)
                o = jnp.dot(p.astype(jnp.bfloat16), vv, preferred_element_type=jnp.float32)
                stats.append((m, l, o))
            m_ref[rows, :] = jnp.where(first, stats[0][0], stats[1][0])
            l_ref[rows, :] = jnp.where(first, stats[0][1], stats[1][1])
            n_ref[rows, :] = jnp.where(first, stats[0][2], stats[1][2])
            return carry

        lax.fori_loop(0, dil * nb, tile, 0, unroll=4)

    chunk = 256

    def merge(c, carry):
        rows = pl.ds(pl.multiple_of(c * chunk, chunk), chunk)
        o_ref[rows, :] = _merge_stats([[stat_refs[3 * g + j][rows, :] for j in range(3)]
                                       for g in range(N_GROUPS)])
        return carry

    lax.fori_loop(0, seq // chunk, merge, 0)


def _prompt_attention(q, k, v, n_seq, seq):
    assert seq % (DIL_PATTERNS[-1][1] * N_BACK) == 0
    in_specs = []
    for g in range(N_GROUPS):
        for _ in range(3):
            in_specs.append(pl.BlockSpec((seq, LANES), lambda b, j, g=g: (b, g * N_SLABS + j)))
    args = []
    for g in range(N_GROUPS):
        args += [q, k, v]
    return pl.pallas_call(
        functools.partial(_prompt_attn_kernel, seq=seq),
        grid=(n_seq, N_SLABS),
        in_specs=in_specs,
        out_specs=pl.BlockSpec((seq, LANES), lambda b, j: (b, j)),
        out_shape=jax.ShapeDtypeStruct((n_seq * seq, ATT_WIDTH), jnp.float32),
        scratch_shapes=[pltpu.VMEM((seq, LANES), jnp.float32)] * (3 * N_GROUPS),
        compiler_params=_params(),
        name="prompt_attention",
    )(*args)


SAMPLE_HEADS_PER_STEP = 4


def _sample_kernel(*refs, t_new):
    qn_ref, kn_ref, vn_ref = refs[:3]
    cache_refs = refs[3:3 + 2 * N_GROUPS]
    out_refs = refs[3 + 2 * N_GROUPS:3 + 4 * N_GROUPS]
    o_ref = refs[3 + 4 * N_GROUPS]
    hb = o_ref.shape[0]
    scale = HEAD_DIM ** -0.5
    lane = lax.broadcasted_iota(jnp.int32, (1, LANES), 1)
    trow = lax.broadcasted_iota(jnp.int32, (t_new, 1), 0)

    def widen(x):
        return jnp.concatenate([x, jnp.zeros((x.shape[0], LANES - t_new), x.dtype)], axis=1)

    def per_head(h, carry):
        head = pl.program_id(1) * hb + h
        score_tiles = []
        for g, (win, dil) in enumerate(DIL_PATTERNS):
            ck = cache_refs[2 * g]
            n_tiles = win // LANES
            per_pass = min(dil, t_new)
            q_t = qn_ref[g, h] * scale
            k_new = widen(kn_ref[g, h])
            slope = _slope_like((1, LANES), g, head)
            res = lane & (dil - 1)
            tiles = [jnp.zeros((t_new, LANES), jnp.float32) for _ in range(n_tiles + 1)]
            for k in range(t_new // per_pass):
                q_mix = None
                for t in range(k * per_pass, (k + 1) * per_pass):
                    q_col = jnp.broadcast_to(q_t[:, t:t + 1], (HEAD_DIM, LANES))
                    row = jnp.sum(k_new * q_col, axis=0, keepdims=True)
                    tiles[n_tiles] = jnp.where(trow == t, row, tiles[n_tiles])
                    term = q_col if per_pass == 1 else jnp.where(res == t % dil, q_col, 0.0)
                    q_mix = term if q_mix is None else q_mix + term
                in_pass = (trow >= k * per_pass) & (trow < (k + 1) * per_pass)
                for j in range(n_tiles):
                    row = jnp.sum(ck[h, :, j * LANES:(j + 1) * LANES] * q_mix, axis=0, keepdims=True)
                    tiles[j] = jnp.where(in_pass, row, tiles[j])
            for j in range(n_tiles + 1):
                if j < n_tiles:
                    dist = win + trow - (j * LANES + lane)
                    valid = ((dist & (dil - 1)) == 0) & (dist <= win)
                else:
                    dist = trow - lane
                    valid = ((dist & (dil - 1)) == 0) & (dist >= 0)
                tiles[j] = jnp.where(valid, tiles[j] - slope * dist.astype(jnp.float32), NEG_INF)
            score_tiles.append(tiles)
        flat = [s for tiles in score_tiles for s in tiles]
        m = functools.reduce(jnp.maximum, [jnp.max(s, axis=1, keepdims=True) for s in flat])
        denom = functools.reduce(lambda a, b: a + b,
                                 [jnp.sum(jnp.exp(s - m), axis=1, keepdims=True) for s in flat])
        probs = [[jnp.exp(s - m) / denom for s in tiles] for tiles in score_tiles]
        pass_sums = {}
        for g, (win, dil) in enumerate(DIL_PATTERNS):
            cv = cache_refs[2 * g + 1]
            per_pass = min(dil, t_new)
            for k in range(t_new // per_pass):
                in_pass = (trow >= k * per_pass) & (trow < (k + 1) * per_pass)
                acc = None
                for j in range(win // LANES):
                    w_row = jnp.sum(jnp.where(in_pass, probs[g][j], 0.0), axis=0, keepdims=True)
                    term = cv[h, :, j * LANES:(j + 1) * LANES] * w_row
                    acc = term if acc is None else acc + term
                pass_sums[g, k] = acc
        out = jnp.zeros((HEAD_DIM, LANES), jnp.float32)
        for t in range(t_new):
            tot = None
            for g, (win, dil) in enumerate(DIL_PATTERNS):
                per_pass = min(dil, t_new)
                mine = pass_sums[g, t // per_pass]
                if per_pass > 1:
                    mine = jnp.where((lane & (dil - 1)) == t % dil, mine, 0.0)
                mine = mine + widen(vn_ref[g, h]) * probs[g][win // LANES][t:t + 1, :]
                tot = mine if tot is None else tot + mine
            out = jnp.where(lane == t, jnp.sum(tot, axis=1, keepdims=True), out)
        o_ref[h] = out[:, 0:t_new]
        for g, (win, dil) in enumerate(DIL_PATTERNS):
            for src, new, dst in ((cache_refs[2 * g], kn_ref, out_refs[2 * g]),
                                  (cache_refs[2 * g + 1], vn_ref, out_refs[2 * g + 1])):
                rolled = pltpu.roll(src[h], win - t_new, 1)
                top = pltpu.roll(widen(new[g, h]), LANES - t_new, 1)
                if win > LANES:
                    dst[h, :, 0:win - LANES] = rolled[:, 0:win - LANES]
                dst[h, :, win - LANES:win] = jnp.where(lane >= LANES - t_new, top,
                                                       rolled[:, win - LANES:win])
        return carry

    lax.fori_loop(0, hb, per_head, 0)


def _sample_attention(q_new, k_new, v_new, caches_k, caches_v):
    n, _, nh, dh, t_new = q_new.shape
    hb = SAMPLE_HEADS_PER_STEP
    new_spec = pl.BlockSpec((None, N_GROUPS, hb, dh, t_new), lambda b, j: (b, 0, j, 0, 0))
    in_specs, args, out_specs, out_shape = [new_spec] * 3, [q_new, k_new, v_new], [], []
    for g, (win, dil) in enumerate(DIL_PATTERNS):
        assert caches_k[g].shape == (n, nh, dh, win) and win % LANES == 0 and dil & (dil - 1) == 0
        spec = pl.BlockSpec((None, hb, dh, win), lambda b, j: (b, j, 0, 0))
        in_specs += [spec, spec]
        args += [caches_k[g], caches_v[g]]
        out_specs += [spec, spec]
        out_shape += [jax.ShapeDtypeStruct(caches_k[g].shape, jnp.float32)] * 2
    out_specs.append(pl.BlockSpec((None, hb, dh, t_new), lambda b, j: (b, j, 0, 0)))
    out_shape.append(jax.ShapeDtypeStruct((n, nh, dh, t_new), jnp.float32))
    res = pl.pallas_call(
        functools.partial(_sample_kernel, t_new=t_new),
        grid=(n, nh // hb),
        in_specs=in_specs,
        out_specs=out_specs,
        out_shape=out_shape,
        compiler_params=_params(),
        name="sample_attention",
    )(*args)
    return res[-1], res[:-1]


def _catnorm_kernel(att_ref, cy_ref, ga_ref, gc_ref, o_ref):
    wa = att_ref.shape[1]
    o_ref[:, 0:wa] = _rms(att_ref[...], ga_ref[...]).astype(o_ref.dtype)
    o_ref[:, wa:] = _rms(cy_ref[...], gc_ref[...]).astype(o_ref.dtype)


def _catnorm(att, cy, g_att, g_conv, tm):
    t, wa = att.shape
    wc = cy.shape[1]
    return pl.pallas_call(
        _catnorm_kernel,
        grid=(t // tm,),
        in_specs=[pl.BlockSpec((tm, wa), lambda i: (i, 0)),
                  pl.BlockSpec((tm, wc), lambda i: (i, 0)),
                  pl.BlockSpec((1, wa), lambda i: (0, 0)),
                  pl.BlockSpec((1, wc), lambda i: (0, 0))],
        out_specs=pl.BlockSpec((tm, wa + wc), lambda i: (i, 0)),
        out_shape=jax.ShapeDtypeStruct((t, wa + wc), jnp.bfloat16),
        compiler_params=_params(),
        name="catnorm",
    )(att, cy, g_att.reshape(1, wa), g_conv.reshape(1, wc))


def _outproj_kernel(a_ref, w_ref, x_ref, g_ref, h_ref, hn_ref):
    h = x_ref[...] + jnp.dot(a_ref[...], w_ref[...], preferred_element_type=jnp.float32)
    h_ref[...] = h
    hn_ref[...] = _rms(h, g_ref[...]).astype(hn_ref.dtype)


def _outproj(cat, w_out, x, g_ffn, tm):
    t, k = cat.shape
    d = w_out.shape[1]
    return pl.pallas_call(
        _outproj_kernel,
        grid=(t // tm,),
        in_specs=[pl.BlockSpec((tm, k), lambda i: (i, 0)),
                  pl.BlockSpec((k, d), lambda i: (0, 0)),
                  pl.BlockSpec((tm, d), lambda i: (i, 0)),
                  pl.BlockSpec((1, d), lambda i: (0, 0))],
        out_specs=[pl.BlockSpec((tm, d), lambda i: (i, 0)),
                   pl.BlockSpec((tm, d), lambda i: (i, 0))],
        out_shape=[jax.ShapeDtypeStruct((t, d), jnp.float32),
                   jax.ShapeDtypeStruct((t, d), jnp.bfloat16)],
        compiler_params=_params(),
        name="outproj",
    )(cat, w_out, x, g_ffn.reshape(1, d))


_N_RANK = PEER_TOPK + 1
_CAND_PER_RANK = tuple(_N_RANK // (i + 1) for i in range(_N_RANK))
_N_CAND = sum(_CAND_PER_RANK)
_CAND_ROWS = -(-_N_CAND // 8) * 8


def _top_values(xs, k):
    tt = xs[0].shape[1]
    slot = lax.broadcasted_iota(jnp.int32, (k, tt), 0)
    rowfs = [lax.broadcasted_iota(jnp.int32, x.shape, 0).astype(jnp.float32) for x in xs]

    def step(r, carry):
        new = []
        for (x, vals), rowf in zip(carry, rowfs):
            m = jnp.max(x, axis=0, keepdims=True)
            first = jnp.min(jnp.where(x == m, rowf, float(x.shape[0])), axis=0, keepdims=True)
            new.append((jnp.where(rowf == first, -jnp.inf, x), jnp.where(slot == r, m, vals)))
        return tuple(new)

    init = tuple((x, jnp.zeros((k, tt), jnp.float32)) for x in xs)
    return [vals for _, vals in lax.fori_loop(0, k, step, init)]


def _merge_desc(xs):
    n = len(xs)
    if n == 1:
        return xs
    half = n // 2
    hi = [jnp.maximum(xs[i], xs[i + half]) for i in range(half)]
    lo = [jnp.minimum(xs[i], xs[i + half]) for i in range(half)]
    return _merge_desc(hi) + _merge_desc(lo)


def _sort_desc(xs):
    n = len(xs)
    if n == 1:
        return xs
    return _merge_desc(_sort_desc(xs[:n // 2]) + _sort_desc(xs[n // 2:])[::-1])


def _top_sorted(x, k):
    rows, tt = x.shape
    keep = 32
    assert rows == 8 * 16 and k <= keep
    outs = []
    for lt in range(tt // LANES):
        xs = _sort_desc([x[8 * i:8 * i + 8, lt * LANES:(lt + 1) * LANES] for i in range(rows // 8)])
        other = [pltpu.roll(v, 4, 0) for v in xs]
        xs = _merge_desc(xs + other[::-1])
        for shift in (2, 1):
            other = [pltpu.roll(v, shift, 0) for v in xs]
            xs = _merge_desc([jnp.maximum(xs[i], other[keep - 1 - i]) for i in range(keep)])
        outs.append(jnp.concatenate([v[0:1, :] for v in xs[:k]], axis=0))
    return jnp.concatenate(outs, axis=1)


def _route_kernel(q_ref, ka_ref, kb_ref, thr_ref, sb_ref, ea_ref, eb_ref):
    ka = ka_ref[...]
    kb = kb_ref[...]
    tt = q_ref.shape[0]
    sub = N_KEYS // 8

    def per_head(h, carry):
        c0 = pl.multiple_of(h * PEER_QDIM, PEER_QDIM)
        qa = q_ref[:, pl.ds(c0, PEER_HALF)]
        qb = q_ref[:, pl.ds(c0 + PEER_HALF, PEER_HALF)]
        sa = lax.dot_general(ka, qa, _NT, preferred_element_type=jnp.float32)
        sb = lax.dot_general(kb, qb, _NT, preferred_element_type=jnp.float32)
        va = _top_sorted(sa, _N_RANK)
        vb = _top_sorted(sb, _N_RANK)
        pieces = [va[i:i + 1, :] + vb[0:n, :] for i, n in enumerate(_CAND_PER_RANK)]
        pieces.append(jnp.full((_CAND_ROWS - _N_CAND, tt), -jnp.inf, jnp.float32))
        (top,) = _top_values((jnp.concatenate(pieces, axis=0),), _N_RANK)
        z = jnp.sum(jnp.exp(top[0:PEER_TOPK, :] - top[0:1, :]), axis=0, keepdims=True)
        tau = 0.5 * (top[PEER_TOPK - 1:PEER_TOPK, :] + top[PEER_TOPK:PEER_TOPK + 1, :])
        thr = tau - sa
        ea = jnp.exp(sa - va[0:1, :])
        eb = jnp.exp(sb - vb[0:1, :]) / z
        for lt in range(tt // LANES):
            ls = slice(lt * LANES, (lt + 1) * LANES)
            thr_ref[h, lt] = thr[:, ls].reshape(sub, 8, LANES)
            ea_ref[h, lt] = ea[:, ls].reshape(sub, 8, LANES)
            sb_ref[h, lt] = sb[:, ls]
            eb_ref[h, lt] = eb[:, ls]
        return carry

    lax.fori_loop(0, PEER_HEADS, per_head, 0)


def _route_specs(tt, imap):
    n_lt = tt // LANES
    row = pl.BlockSpec((PEER_HEADS, n_lt, N_KEYS // 8, 8, LANES), lambda *g: (0, imap(*g), 0, 0, 0))
    full = pl.BlockSpec((PEER_HEADS, n_lt, N_KEYS, LANES), lambda *g: (0, imap(*g), 0, 0))
    return [row, full, row, full]


def _route(qp, ka, kb, tt):
    t = qp.shape[0]
    row_shape = jax.ShapeDtypeStruct((PEER_HEADS, t // LANES, N_KEYS // 8, 8, LANES), jnp.float32)
    full_shape = jax.ShapeDtypeStruct((PEER_HEADS, t // LANES, N_KEYS, LANES), jnp.float32)
    return pl.pallas_call(
        _route_kernel,
        grid=(t // tt,),
        in_specs=[pl.BlockSpec((tt, PEER_HEADS * PEER_QDIM), lambda i: (i, 0)),
                  pl.BlockSpec((N_KEYS, PEER_HALF), lambda i: (0, 0)),
                  pl.BlockSpec((N_KEYS, PEER_HALF), lambda i: (0, 0))],
        out_specs=_route_specs(tt, lambda i: i),
        out_shape=[row_shape, full_shape, row_shape, full_shape],
        compiler_params=_params(),
        name="peer_route",
    )(qp, ka, kb)


def _gelu(x):
    return 0.5 * x * (1.0 + lax.erf(x * math.sqrt(0.5)))


def _peer_kernel(hn_ref, u_ref, v_ref, thr_ref, sb_ref, ea_ref, eb_ref, o_ref, h_scr, w_scr):
    e = pl.program_id(1)
    et, tt = h_scr.shape
    assert et == 8 * N_KEYS

    @pl.when(e == 0)
    def _():
        o_ref[...] = jnp.zeros(o_ref.shape, o_ref.dtype)

    h_scr[...] = lax.dot_general(u_ref[...], hn_ref[...], _NT,
                                 preferred_element_type=jnp.float32)

    def per_lane_tile(lt, carry):
        ls = pl.ds(pl.multiple_of(lt * LANES, LANES), LANES)
        for a in range(et // N_KEYS):
            rows = pl.ds(a * N_KEYS, N_KEYS)
            acc = None
            for h in range(PEER_HEADS):
                keep = sb_ref[h, lt] >= thr_ref[h, lt, e, a:a + 1, :]
                gate = jnp.where(keep, eb_ref[h, lt], 0.0) * ea_ref[h, lt, e, a:a + 1, :]
                acc = gate if acc is None else acc + gate
            w_scr[rows, ls] = (acc * _gelu(h_scr[rows, ls])).astype(w_scr.dtype)
        return carry

    lax.fori_loop(0, tt // LANES, per_lane_tile, 0)
    o_ref[...] += lax.dot_general(w_scr[...], v_ref[...], _TN,
                                  preferred_element_type=jnp.float32)


def _peer_dense(hn, u, v, routing, tt, et):
    t, d = hn.shape
    n_exp = u.shape[0]
    return pl.pallas_call(
        _peer_kernel,
        grid=(t // tt, n_exp // et),
        in_specs=[pl.BlockSpec((tt, d), lambda i, e: (i, 0)),
                  pl.BlockSpec((et, d), lambda i, e: (e, 0)),
                  pl.BlockSpec((et, d), lambda i, e: (e, 0))] + _route_specs(tt, lambda i, e: i),
        out_specs=pl.BlockSpec((tt, d), lambda i, e: (i, 0)),
        out_shape=jax.ShapeDtypeStruct((t, d), jnp.float32),
        scratch_shapes=[pltpu.VMEM((et, tt), jnp.float32),
                        pltpu.VMEM((et, tt), jnp.bfloat16)],
        compiler_params=_params(),
        name="peer_dense",
    )(hn, u, v, *routing)


def _final_kernel(h_ref, p_ref, g_ref, o_ref):
    o_ref[...] = _rms(h_ref[...] + p_ref[...], g_ref[...])


def _final(h, p, g, tm):
    t, d = h.shape
    row = pl.BlockSpec((tm, d), lambda i: (i, 0))
    return pl.pallas_call(
        _final_kernel,
        grid=(t // tm,),
        in_specs=[row, row, pl.BlockSpec((1, d), lambda i: (0, 0))],
        out_specs=row,
        out_shape=jax.ShapeDtypeStruct((t, d), jnp.float32),
        compiler_params=_params(),
        name="final_norm",
    )(h, p, g.reshape(1, d))


def _tile(t, pref):
    while t % pref:
        pref //= 2
    return pref


def _project_in(x2, norm_g, w_in_bf):
    t = x2.shape[0]
    xn = _rmsnorm(x2, norm_g, jnp.bfloat16, _tile(t, 256))
    tm = _tile(t, 1024)
    cw = (w_in_bf.shape[1] - 3 * QKV_WIDTH)
    q = _matmul(xn, w_in_bf, 0, QKV_WIDTH, jnp.float32, tm, 1024)
    k = _matmul(xn, w_in_bf, QKV_WIDTH, QKV_WIDTH, jnp.float32, tm, 1024)
    v = _matmul(xn, w_in_bf, 2 * QKV_WIDTH, QKV_WIDTH, jnp.float32, tm, 1024)
    pc = _matmul(xn, w_in_bf, 3 * QKV_WIDTH, cw, jnp.float32, tm, 1024)
    return q, k, v, pc, xn


def _channel_mix(x2, att, cy, lw):
    t = x2.shape[0]
    cat = _catnorm(att, cy, lw["g_att"], lw["g_conv"], _tile(t, 256))
    h, hn = _outproj(cat, lw["w_out"], x2, lw["g_ffn"], _tile(t, 256))
    qp = _matmul(hn, lw["w_pq"], 0, lw["w_pq"].shape[1], jnp.bfloat16, _tile(t, 1024), 1024)
    routing = _route(qp, lw["ka"], lw["kb"], _tile(t, 512))
    peer = _peer_dense(hn, lw["u"], lw["v"], routing, _tile(t, 512), 8 * N_KEYS)
    return h, peer


def kernel(x_prompt, x_sample, cache_k_g0, cache_v_g0, cache_k_g1, cache_v_g1, cache_k_g2, cache_v_g2, state_conv, norm_mix, w_in, conv_w, norm_att_out, norm_conv_out, w_out, norm_ffn, w_pq, sub_keys_a, sub_keys_b, expert_u, expert_v, norm_final):
    bf = jnp.bfloat16
    depth = w_in.shape[0]
    b, s, d = x_prompt.shape
    bd, td, _ = x_sample.shape
    caches_k = (cache_k_g0, cache_k_g1, cache_k_g2)
    caches_v = (cache_v_g0, cache_v_g1, cache_v_g2)
    conv_c = conv_w.shape[2]

    hp = x_prompt.reshape(b * s, d)
    hs = x_sample.reshape(bd * td, d)
    nk_p = [[] for _ in range(N_GROUPS)]
    nv_p = [[] for _ in range(N_GROUPS)]
    nk_s = [[] for _ in range(N_GROUPS)]
    nv_s = [[] for _ in range(N_GROUPS)]
    nc_p, nc_s = [], []
    for l in range(depth):
        lw = dict(g_att=norm_att_out[l], g_conv=norm_conv_out[l], w_out=w_out[l].astype(bf),
                  g_ffn=norm_ffn[l], w_pq=w_pq[l].astype(bf), ka=sub_keys_a[l].astype(bf),
                  kb=sub_keys_b[l].astype(bf), u=expert_u[l].astype(bf), v=expert_v[l].astype(bf))
        w_in_bf = w_in[l].astype(bf)

        q, k, v, pc, xn = _project_in(hp, norm_mix[l], w_in_bf)
        att = _prompt_attention(q, k, v, b, s)
        cy, ns = _gated_conv(pc, jnp.zeros((b, CONV_K - 1, conv_c), jnp.float32), conv_w[l], b, s, 256)
        for g, (win, _) in enumerate(DIL_PATTERNS):
            rows = min(win, s)
            for dst, base in ((nk_p, QKV_WIDTH), (nv_p, 2 * QKV_WIDTH)):
                proj_t = _matmul_t(w_in_bf, xn, base + g * ATT_WIDTH, ATT_WIDTH, b, s, rows, 512)
                dst[g].append(jnp.transpose(proj_t.reshape(b, ATT_SLOTS, HEAD_DIM, rows), (0, 3, 1, 2)))
        nc_p.append(ns)
        h_res, peer = _channel_mix(hp, att, cy, lw)
        last = l == depth - 1
        if last:
            y_prompt = _final(h_res, peer, norm_final, _tile(b * s, 256))
        else:
            hp = h_res + peer

        q, k, v, pc, _ = _project_in(hs, norm_mix[l], w_in_bf)
        new_t = lambda a: jnp.transpose(a.reshape(bd, td, N_GROUPS, ATT_SLOTS, HEAD_DIM), (0, 2, 3, 4, 1))
        ck_t = [jnp.transpose(c[l], (0, 2, 3, 1)) for c in caches_k]
        cv_t = [jnp.transpose(c[l], (0, 2, 3, 1)) for c in caches_v]
        att_t, rolled = _sample_attention(new_t(q), new_t(k), new_t(v), ck_t, cv_t)
        att = jnp.transpose(att_t, (0, 3, 1, 2)).reshape(bd * td, ATT_WIDTH)
        for g in range(N_GROUPS):
            nk_s[g].append(jnp.transpose(rolled[2 * g], (0, 3, 1, 2)))
            nv_s[g].append(jnp.transpose(rolled[2 * g + 1], (0, 3, 1, 2)))
        cy, ns = _gated_conv(pc, state_conv[l], conv_w[l], bd, td, conv_c)
        nc_s.append(ns)
        h_res, peer = _channel_mix(hs, att, cy, lw)
        if last:
            y_sample = _final(h_res, peer, norm_final, _tile(bd * td, 256))
        else:
            hs = h_res + peer

    return (y_prompt.reshape(b, s, d), y_sample.reshape(bd, td, d),
            jnp.stack(nk_p[0], 0), jnp.stack(nv_p[0], 0),
            jnp.stack(nk_p[1], 0), jnp.stack(nv_p[1], 0),
            jnp.stack(nk_p[2], 0), jnp.stack(nv_p[2], 0),
            jnp.stack(nc_p, 0),
            jnp.stack(nk_s[0], 0), jnp.stack(nv_s[0], 0),
            jnp.stack(nk_s[1], 0), jnp.stack(nv_s[1], 0),
            jnp.stack(nk_s[2], 0), jnp.stack(nv_s[2], 0),
            jnp.stack(nc_s, 0))
```

```python
import functools
import math

import jax
import jax.numpy as jnp
from jax import lax
from jax.experimental import pallas as pl
from jax.experimental.pallas import tpu as pltpu

HEAD_DIM = 64
ATT_SLOTS = 16
DIL_PATTERNS = ((128, 1), (512, 4), (2048, 16))
N_GROUPS = len(DIL_PATTERNS)
ATT_WIDTH = ATT_SLOTS * HEAD_DIM
QKV_WIDTH = N_GROUPS * ATT_WIDTH
CONV_K = 3
N_KEYS = 128
PEER_HEADS = 8
PEER_QDIM = 256
PEER_HALF = PEER_QDIM // 2
PEER_TOPK = 16
NORM_EPS = 1e-6
NEG_INF = -1e30

LANES = 128
HEADS_PER_SLAB = LANES // HEAD_DIM
N_SLABS = ATT_WIDTH // LANES
N_BACK = 128
VMEM_LIMIT = 56 * 1024 * 1024

_NT = (((1,), (1,)), ((), ()))
_TN = (((0,), (0,)), ((), ()))


def _params(vmem=None):
    return pltpu.CompilerParams(vmem_limit_bytes=vmem or VMEM_LIMIT)


def _rms(x, g):
    return x * lax.rsqrt(jnp.mean(x * x, axis=-1, keepdims=True) + NORM_EPS) * g


def _rmsnorm_kernel(x_ref, g_ref, o_ref):
    o_ref[...] = _rms(x_ref[...], g_ref[...]).astype(o_ref.dtype)


def _rmsnorm(x, g, out_dtype, tm):
    t, d = x.shape
    return pl.pallas_call(
        _rmsnorm_kernel,
        grid=(t // tm,),
        in_specs=[pl.BlockSpec((tm, d), lambda i: (i, 0)),
                  pl.BlockSpec((1, d), lambda i: (0, 0))],
        out_specs=pl.BlockSpec((tm, d), lambda i: (i, 0)),
        out_shape=jax.ShapeDtypeStruct((t, d), out_dtype),
        compiler_params=_params(),
        name="rmsnorm",
    )(x, g.reshape(1, d))


def _mm_kernel(a_ref, b_ref, o_ref):
    o_ref[...] = jnp.dot(a_ref[...], b_ref[...],
                         preferred_element_type=jnp.float32).astype(o_ref.dtype)


def _matmul(a, b, col_start, n_cols, out_dtype, tm, tn):
    t, k = a.shape
    off = col_start // tn
    return pl.pallas_call(
        _mm_kernel,
        grid=(t // tm, n_cols // tn),
        in_specs=[pl.BlockSpec((tm, k), lambda i, j: (i, 0)),
                  pl.BlockSpec((k, tn), lambda i, j: (0, j + off))],
        out_specs=pl.BlockSpec((tm, tn), lambda i, j: (i, j)),
        out_shape=jax.ShapeDtypeStruct((t, n_cols), out_dtype),
        compiler_params=_params(),
        name="matmul",
    )(a, b)


def _mm_t_kernel(w_ref, x_ref, o_ref):
    o_ref[...] = lax.dot_general(w_ref[...], x_ref[...], (((0,), (1,)), ((), ())),
                                 preferred_element_type=jnp.float32)


def _matmul_t(w, x, col_start, n_cols, n_seq, seq, rows, tn):
    k = x.shape[1]
    per = seq // rows
    off = col_start // tn
    return pl.pallas_call(
        _mm_t_kernel,
        grid=(n_seq, n_cols // tn),
        in_specs=[pl.BlockSpec((k, tn), lambda b, j: (0, j + off)),
                  pl.BlockSpec((rows, k), lambda b, j: (b * per + per - 1, 0))],
        out_specs=pl.BlockSpec((None, tn, rows), lambda b, j: (b, j, 0)),
        out_shape=jax.ShapeDtypeStruct((n_seq, n_cols, rows), jnp.float32),
        compiler_params=_params(),
        name="matmul_t",
    )(w, x)


def _conv_kernel(gb_ref, gc_ref, h_ref, st_ref, w_ref, y_ref, ns_ref):
    u = gc_ref[...] * h_ref[...]
    s = u.shape[0]
    row = lax.broadcasted_iota(jnp.int32, u.shape, 0)
    st = st_ref[...]
    w = w_ref[...]
    u1 = jnp.where(row == 0, st[1:2, :], pltpu.roll(u, 1, 0))
    u2 = jnp.where(row == 0, st[0:1, :],
                   jnp.where(row == 1, st[1:2, :], pltpu.roll(u, 2, 0)))
    y = u2 * w[0:1, :]
    y = y + u1 * w[1:2, :]
    y = y + u * w[2:3, :]
    y_ref[...] = gb_ref[...] * y
    ns_ref[...] = u[s - 2:s, :]


def _gated_conv(pc, state, conv_w, n_seq, seq, cb):
    c = conv_w.shape[1]
    ncb = c // cb
    return pl.pallas_call(
        _conv_kernel,
        grid=(n_seq, ncb),
        in_specs=[pl.BlockSpec((seq, cb), lambda b, j: (b, j)),
                  pl.BlockSpec((seq, cb), lambda b, j: (b, ncb + j)),
                  pl.BlockSpec((seq, cb), lambda b, j: (b, 2 * ncb + j)),
                  pl.BlockSpec((None, CONV_K - 1, cb), lambda b, j: (b, 0, j)),
                  pl.BlockSpec((CONV_K, cb), lambda b, j: (0, j))],
        out_specs=[pl.BlockSpec((seq, cb), lambda b, j: (b, j)),
                   pl.BlockSpec((None, CONV_K - 1, cb), lambda b, j: (b, 0, j))],
        out_shape=[jax.ShapeDtypeStruct((n_seq * seq, c), jnp.float32),
                   jax.ShapeDtypeStruct((n_seq, CONV_K - 1, c), jnp.float32)],
        compiler_params=_params(),
        name="gated_conv",
    )(pc, pc, pc, state, conv_w)


def _slope_like(shape, group, head):
    i = (group * ATT_SLOTS + 1 + head).astype(jnp.float32)
    n = float(N_GROUPS * ATT_SLOTS)
    return jnp.exp2(jnp.broadcast_to(-8.0 * i / n, shape))


def _merge_stats(stats):
    ms = [s[0] for s in stats]
    m = functools.reduce(jnp.maximum, ms)
    den = num = None
    for (mg, lg, ng) in stats:
        a = jnp.exp(mg - m)
        den = a * lg if den is None else den + a * lg
        num = a * ng if num is None else num + a * ng
    return num / den


def _prompt_attn_kernel(q0, k0, v0, q1, k1, v1, q2, k2, v2, o_ref, *stat_refs, seq):
    slab = pl.program_id(1)
    lane = lax.broadcasted_iota(jnp.int32, (1, LANES), 1)
    first = lane < HEAD_DIM
    scale = HEAD_DIM ** -0.5

    refs = ((q0, k0, v0), (q1, k1, v1), (q2, k2, v2))
    for g, (win, dil) in enumerate(DIL_PATTERNS):
        q_ref, k_ref, v_ref = refs[g]
        m_ref, l_ref, n_ref = stat_refs[3 * g:3 * g + 3]
        span = dil * N_BACK
        nb = seq // span
        two = nb > 1
        nk = (2 if two else 1) * N_BACK
        qi = lax.broadcasted_iota(jnp.int32, (N_BACK, nk), 0)
        kc = lax.broadcasted_iota(jnp.int32, (N_BACK, nk), 1)
        steps = (N_BACK if two else 0) + qi - kc
        in_band = (steps >= 0) & (steps <= N_BACK)
        dist = (steps * dil).astype(jnp.float32)
        slopes = [_slope_like((1, nk), g, slab * HEADS_PER_SLAB + h)
                  for h in range(HEADS_PER_SLAB)]

        def tile(t, carry, q_ref=q_ref, k_ref=k_ref, v_ref=v_ref, m_ref=m_ref, l_ref=l_ref,
                 n_ref=n_ref, dil=dil, span=span, nb=nb, two=two, kc=kc, in_band=in_band,
                 dist=dist, slopes=slopes):
            r = t // nb
            n = t - r * nb
            start = n * span + r
            rows = pl.ds(start, N_BACK, stride=dil)
            q = q_ref[rows, :] * scale
            if two:
                prev = jnp.maximum(start - span, r)
                prows = pl.ds(prev, N_BACK, stride=dil)
                kk = jnp.concatenate([k_ref[prows, :], k_ref[rows, :]], axis=0)
                vv = jnp.concatenate([v_ref[prows, :], v_ref[rows, :]], axis=0)
                valid = in_band & ((kc >= N_BACK) | (n > 0))
            else:
                kk, vv, valid = k_ref[rows, :], v_ref[rows, :], in_band
            kk = kk.astype(jnp.bfloat16)
            vv = vv.astype(jnp.bfloat16)
            stats = []
            for h in range(HEADS_PER_SLAB):
                mine = first if h == 0 else jnp.logical_not(first)
                qm = jnp.where(mine, q, 0.0).astype(jnp.bfloat16)
                s = lax.dot_general(qm, kk, _NT, preferred_element_type=jnp.float32)
                s = s + jnp.where(valid, -slopes[h] * dist, NEG_INF)
                m = jnp.max(s, axis=-1, keepdims=True)
                p = jnp.exp(s - m)
                l = jnp.sum(p, axis=-1, keepdims=True)
                o = jnp.dot(p.astype(jnp.bfloat16), vv, preferred_element_type=jnp.float32)
                stats.append((m, l, o))
            m_ref[rows, :] = jnp.where(first, stats[0][0], stats[1][0])
            l_ref[rows, :] = jnp.where(first, stats[0][1], stats[1][1])
            n_ref[rows, :] = jnp.where(first, stats[0][2], stats[1][2])
            return carry

        lax.fori_loop(0, dil * nb, tile, 0, unroll=8)

    chunk = 256

    def merge(c, carry):
        rows = pl.ds(pl.multiple_of(c * chunk, chunk), chunk)
        o_ref[rows, :] = _merge_stats([[stat_refs[3 * g + j][rows, :] for j in range(3)]
                                       for g in range(N_GROUPS)])
        return carry

    lax.fori_loop(0, seq // chunk, merge, 0)


def _prompt_attention(q, k, v, n_seq, seq):
    assert seq % (DIL_PATTERNS[-1][1] * N_BACK) == 0
    in_specs = []
    for g in range(N_GROUPS):
        for _ in range(3):
            in_specs.append(pl.BlockSpec((seq, LANES), lambda b, j, g=g: (b, g * N_SLABS + j)))
    args = []
    for g in range(N_GROUPS):
        args += [q, k, v]
    return pl.pallas_call(
        functools.partial(_prompt_attn_kernel, seq=seq),
        grid=(n_seq, N_SLABS),
        in_specs=in_specs,
        out_specs=pl.BlockSpec((seq, LANES), lambda b, j: (b, j)),
        out_shape=jax.ShapeDtypeStruct((n_seq * seq, ATT_WIDTH), jnp.float32),
        scratch_shapes=[pltpu.VMEM((seq, LANES), jnp.float32)] * (3 * N_GROUPS),
        compiler_params=_params(),
        name="prompt_attention",
    )(*args)


SAMPLE_HEADS_PER_STEP = 4


def _sample_kernel(*refs, t_new):
    qn_ref, kn_ref, vn_ref = refs[:3]
    cache_refs = refs[3:3 + 2 * N_GROUPS]
    out_refs = refs[3 + 2 * N_GROUPS:3 + 4 * N_GROUPS]
    o_ref = refs[3 + 4 * N_GROUPS]
    hb = o_ref.shape[0]
    scale = HEAD_DIM ** -0.5
    lane = lax.broadcasted_iota(jnp.int32, (1, LANES), 1)
    trow = lax.broadcasted_iota(jnp.int32, (t_new, 1), 0)

    def widen(x):
        return jnp.concatenate([x, jnp.zeros((x.shape[0], LANES - t_new), x.dtype)], axis=1)

    def per_head(h, carry):
        head = pl.program_id(1) * hb + h
        score_tiles = []
        for g, (win, dil) in enumerate(DIL_PATTERNS):
            ck = cache_refs[2 * g]
            n_tiles = win // LANES
            per_pass = min(dil, t_new)
            q_t = qn_ref[g, h] * scale
            k_new = widen(kn_ref[g, h])
            slope = _slope_like((1, LANES), g, head)
            res = lane & (dil - 1)
            tiles = [jnp.zeros((t_new, LANES), jnp.float32) for _ in range(n_tiles + 1)]
            for k in range(t_new // per_pass):
                q_mix = None
                for t in range(k * per_pass, (k + 1) * per_pass):
                    q_col = jnp.broadcast_to(q_t[:, t:t + 1], (HEAD_DIM, LANES))
                    row = jnp.sum(k_new * q_col, axis=0, keepdims=True)
                    tiles[n_tiles] = jnp.where(trow == t, row, tiles[n_tiles])
                    term = q_col if per_pass == 1 else jnp.where(res == t % dil, q_col, 0.0)
                    q_mix = term if q_mix is None else q_mix + term
                in_pass = (trow >= k * per_pass) & (trow < (k + 1) * per_pass)
                for j in range(n_tiles):
                    row = jnp.sum(ck[h, :, j * LANES:(j + 1) * LANES] * q_mix, axis=0, keepdims=True)
                    tiles[j] = jnp.where(in_pass, row, tiles[j])
            for j in range(n_tiles + 1):
                if j < n_tiles:
                    dist = win + trow - (j * LANES + lane)
                    valid = ((dist & (dil - 1)) == 0) & (dist <= win)
                else:
                    dist = trow - lane
                    valid = ((dist & (dil - 1)) == 0) & (dist >= 0)
                tiles[j] = jnp.where(valid, tiles[j] - slope * dist.astype(jnp.float32), NEG_INF)
            score_tiles.append(tiles)
        flat = [s for tiles in score_tiles for s in tiles]
        m = functools.reduce(jnp.maximum, [jnp.max(s, axis=1, keepdims=True) for s in flat])
        denom = functools.reduce(lambda a, b: a + b,
                                 [jnp.sum(jnp.exp(s - m), axis=1, keepdims=True) for s in flat])
        probs = [[jnp.exp(s - m) / denom for s in tiles] for tiles in score_tiles]
        pass_sums = {}
        for g, (win, dil) in enumerate(DIL_PATTERNS):
            cv = cache_refs[2 * g + 1]
            per_pass = min(dil, t_new)
            for k in range(t_new // per_pass):
                in_pass = (trow >= k * per_pass) & (trow < (k + 1) * per_pass)
                acc = None
                for j in range(win // LANES):
                    w_row = jnp.sum(jnp.where(in_pass, probs[g][j], 0.0), axis=0, keepdims=True)
                    term = cv[h, :, j * LANES:(j + 1) * LANES] * w_row
                    acc = term if acc is None else acc + term
                pass_sums[g, k] = acc
        out = jnp.zeros((HEAD_DIM, LANES), jnp.float32)
        for t in range(t_new):
            tot = None
            for g, (win, dil) in enumerate(DIL_PATTERNS):
                per_pass = min(dil, t_new)
                mine = pass_sums[g, t // per_pass]
                if per_pass > 1:
                    mine = jnp.where((lane & (dil - 1)) == t % dil, mine, 0.0)
                mine = mine + widen(vn_ref[g, h]) * probs[g][win // LANES][t:t + 1, :]
                tot = mine if tot is None else tot + mine
            out = jnp.where(lane == t, jnp.sum(tot, axis=1, keepdims=True), out)
        o_ref[h] = out[:, 0:t_new]
        for g, (win, dil) in enumerate(DIL_PATTERNS):
            for src, new, dst in ((cache_refs[2 * g], kn_ref, out_refs[2 * g]),
                                  (cache_refs[2 * g + 1], vn_ref, out_refs[2 * g + 1])):
                rolled = pltpu.roll(src[h], win - t_new, 1)
                top = pltpu.roll(widen(new[g, h]), LANES - t_new, 1)
                if win > LANES:
                    dst[h, :, 0:win - LANES] = rolled[:, 0:win - LANES]
                dst[h, :, win - LANES:win] = jnp.where(lane >= LANES - t_new, top,
                                                       rolled[:, win - LANES:win])
        return carry

    lax.fori_loop(0, hb, per_head, 0)


def _sample_attention(q_new, k_new, v_new, caches_k, caches_v):
    n, _, nh, dh, t_new = q_new.shape
    hb = SAMPLE_HEADS_PER_STEP
    new_spec = pl.BlockSpec((None, N_GROUPS, hb, dh, t_new), lambda b, j: (b, 0, j, 0, 0))
    in_specs, args, out_specs, out_shape = [new_spec] * 3, [q_new, k_new, v_new], [], []
    for g, (win, dil) in enumerate(DIL_PATTERNS):
        assert caches_k[g].shape == (n, nh, dh, win) and win % LANES == 0 and dil & (dil - 1) == 0
        spec = pl.BlockSpec((None, hb, dh, win), lambda b, j: (b, j, 0, 0))
        in_specs += [spec, spec]
        args += [caches_k[g], caches_v[g]]
        out_specs += [spec, spec]
        out_shape += [jax.ShapeDtypeStruct(caches_k[g].shape, jnp.float32)] * 2
    out_specs.append(pl.BlockSpec((None, hb, dh, t_new), lambda b, j: (b, j, 0, 0)))
    out_shape.append(jax.ShapeDtypeStruct((n, nh, dh, t_new), jnp.float32))
    res = pl.pallas_call(
        functools.partial(_sample_kernel, t_new=t_new),
        grid=(n, nh // hb),
        in_specs=in_specs,
        out_specs=out_specs,
        out_shape=out_shape,
        compiler_params=_params(),
        name="sample_attention",
    )(*args)
    return res[-1], res[:-1]


def _catnorm_kernel(att_ref, cy_ref, ga_ref, gc_ref, o_ref):
    wa = att_ref.shape[1]
    o_ref[:, 0:wa] = _rms(att_ref[...], ga_ref[...]).astype(o_ref.dtype)
    o_ref[:, wa:] = _rms(cy_ref[...], gc_ref[...]).astype(o_ref.dtype)


def _catnorm(att, cy, g_att, g_conv, tm):
    t, wa = att.shape
    wc = cy.shape[1]
    return pl.pallas_call(
        _catnorm_kernel,
        grid=(t // tm,),
        in_specs=[pl.BlockSpec((tm, wa), lambda i: (i, 0)),
                  pl.BlockSpec((tm, wc), lambda i: (i, 0)),
                  pl.BlockSpec((1, wa), lambda i: (0, 0)),
                  pl.BlockSpec((1, wc), lambda i: (0, 0))],
        out_specs=pl.BlockSpec((tm, wa + wc), lambda i: (i, 0)),
        out_shape=jax.ShapeDtypeStruct((t, wa + wc), jnp.bfloat16),
        compiler_params=_params(),
        name="catnorm",
    )(att, cy, g_att.reshape(1, wa), g_conv.reshape(1, wc))


def _outproj_kernel(a_ref, w_ref, x_ref, g_ref, h_ref, hn_ref):
    h = x_ref[...] + jnp.dot(a_ref[...], w_ref[...], preferred_element_type=jnp.float32)
    h_ref[...] = h
    hn_ref[...] = _rms(h, g_ref[...]).astype(hn_ref.dtype)


def _outproj(cat, w_out, x, g_ffn, tm):
    t, k = cat.shape
    d = w_out.shape[1]
    return pl.pallas_call(
        _outproj_kernel,
        grid=(t // tm,),
        in_specs=[pl.BlockSpec((tm, k), lambda i: (i, 0)),
                  pl.BlockSpec((k, d), lambda i: (0, 0)),
                  pl.BlockSpec((tm, d), lambda i: (i, 0)),
                  pl.BlockSpec((1, d), lambda i: (0, 0))],
        out_specs=[pl.BlockSpec((tm, d), lambda i: (i, 0)),
                   pl.BlockSpec((tm, d), lambda i: (i, 0))],
        out_shape=[jax.ShapeDtypeStruct((t, d), jnp.float32),
                   jax.ShapeDtypeStruct((t, d), jnp.bfloat16)],
        compiler_params=_params(),
        name="outproj",
    )(cat, w_out, x, g_ffn.reshape(1, d))


_N_RANK = PEER_TOPK + 1
_CAND_PER_RANK = tuple(_N_RANK // (i + 1) for i in range(_N_RANK))
_N_CAND = sum(_CAND_PER_RANK)
_CAND_ROWS = -(-_N_CAND // 8) * 8


def _top_values(xs, k):
    tt = xs[0].shape[1]
    slot = lax.broadcasted_iota(jnp.int32, (k, tt), 0)
    rowfs = [lax.broadcasted_iota(jnp.int32, x.shape, 0).astype(jnp.float32) for x in xs]

    def step(r, carry):
        new = []
        for (x, vals), rowf in zip(carry, rowfs):
            m = jnp.max(x, axis=0, keepdims=True)
            first = jnp.min(jnp.where(x == m, rowf, float(x.shape[0])), axis=0, keepdims=True)
            new.append((jnp.where(rowf == first, -jnp.inf, x), jnp.where(slot == r, m, vals)))
        return tuple(new)

    init = tuple((x, jnp.zeros((k, tt), jnp.float32)) for x in xs)
    return [vals for _, vals in lax.fori_loop(0, k, step, init)]


def _merge_desc(xs):
    n = len(xs)
    if n == 1:
        return xs
    half = n // 2
    hi = [jnp.maximum(xs[i], xs[i + half]) for i in range(half)]
    lo = [jnp.minimum(xs[i], xs[i + half]) for i in range(half)]
    return _merge_desc(hi) + _merge_desc(lo)


def _sort_desc(xs):
    n = len(xs)
    if n == 1:
        return xs
    return _merge_desc(_sort_desc(xs[:n // 2]) + _sort_desc(xs[n // 2:])[::-1])


def _top_sorted(x, k):
    rows, tt = x.shape
    keep = 32
    assert rows == 8 * 16 and k <= keep
    outs = []
    for lt in range(tt // LANES):
        xs = _sort_desc([x[8 * i:8 * i + 8, lt * LANES:(lt + 1) * LANES] for i in range(rows // 8)])
        other = [pltpu.roll(v, 4, 0) for v in xs]
        xs = _merge_desc(xs + other[::-1])
        for shift in (2, 1):
            other = [pltpu.roll(v, shift, 0) for v in xs]
            xs = _merge_desc([jnp.maximum(xs[i], other[keep - 1 - i]) for i in range(keep)])
        outs.append(jnp.concatenate([v[0:1, :] for v in xs[:k]], axis=0))
    return jnp.concatenate(outs, axis=1)


def _route_kernel(q_ref, ka_ref, kb_ref, thr_ref, sb_ref, ea_ref, eb_ref):
    ka = ka_ref[...]
    kb = kb_ref[...]
    tt = q_ref.shape[0]
    sub = N_KEYS // 8

    def per_head(h, carry):
        c0 = pl.multiple_of(h * PEER_QDIM, PEER_QDIM)
        qa = q_ref[:, pl.ds(c0, PEER_HALF)]
        qb = q_ref[:, pl.ds(c0 + PEER_HALF, PEER_HALF)]
        sa = lax.dot_general(ka, qa, _NT, preferred_element_type=jnp.float32)
        sb = lax.dot_general(kb, qb, _NT, preferred_element_type=jnp.float32)
        va = _top_sorted(sa, _N_RANK)
        vb = _top_sorted(sb, _N_RANK)
        pieces = [va[i:i + 1, :] + vb[0:n, :] for i, n in enumerate(_CAND_PER_RANK)]
        pieces.append(jnp.full((_CAND_ROWS - _N_CAND, tt), -jnp.inf, jnp.float32))
        (top,) = _top_values((jnp.concatenate(pieces, axis=0),), _N_RANK)
        z = jnp.sum(jnp.exp(top[0:PEER_TOPK, :] - top[0:1, :]), axis=0, keepdims=True)
        tau = 0.5 * (top[PEER_TOPK - 1:PEER_TOPK, :] + top[PEER_TOPK:PEER_TOPK + 1, :])
        thr = tau - sa
        ea = jnp.exp(sa - va[0:1, :])
        eb = jnp.exp(sb - vb[0:1, :]) / z
        for lt in range(tt // LANES):
            ls = slice(lt * LANES, (lt + 1) * LANES)
            thr_ref[h, lt] = thr[:, ls].reshape(sub, 8, LANES)
            ea_ref[h, lt] = ea[:, ls].reshape(sub, 8, LANES)
            sb_ref[h, lt] = sb[:, ls]
            eb_ref[h, lt] = eb[:, ls]
        return carry

    lax.fori_loop(0, PEER_HEADS, per_head, 0)


def _route_specs(tt, imap):
    n_lt = tt // LANES
    row = pl.BlockSpec((PEER_HEADS, n_lt, N_KEYS // 8, 8, LANES), lambda *g: (0, imap(*g), 0, 0, 0))
    full = pl.BlockSpec((PEER_HEADS, n_lt, N_KEYS, LANES), lambda *g: (0, imap(*g), 0, 0))
    return [row, full, row, full]


def _route(qp, ka, kb, tt):
    t = qp.shape[0]
    row_shape = jax.ShapeDtypeStruct((PEER_HEADS, t // LANES, N_KEYS // 8, 8, LANES), jnp.float32)
    full_shape = jax.ShapeDtypeStruct((PEER_HEADS, t // LANES, N_KEYS, LANES), jnp.float32)
    return pl.pallas_call(
        _route_kernel,
        grid=(t // tt,),
        in_specs=[pl.BlockSpec((tt, PEER_HEADS * PEER_QDIM), lambda i: (i, 0)),
                  pl.BlockSpec((N_KEYS, PEER_HALF), lambda i: (0, 0)),
                  pl.BlockSpec((N_KEYS, PEER_HALF), lambda i: (0, 0))],
        out_specs=_route_specs(tt, lambda i: i),
        out_shape=[row_shape, full_shape, row_shape, full_shape],
        compiler_params=_params(),
        name="peer_route",
    )(qp, ka, kb)


def _gelu(x):
    return 0.5 * x * (1.0 + lax.erf(x * math.sqrt(0.5)))


def _peer_kernel(hn_ref, u_ref, v_ref, thr_ref, sb_ref, ea_ref, eb_ref, o_ref, h_scr, w_scr):
    e = pl.program_id(1)
    et, tt = h_scr.shape
    assert et == 8 * N_KEYS

    @pl.when(e == 0)
    def _():
        o_ref[...] = jnp.zeros(o_ref.shape, o_ref.dtype)

    h_scr[...] = lax.dot_general(u_ref[...], hn_ref[...], _NT,
                                 preferred_element_type=jnp.float32)

    def per_lane_tile(lt, carry):
        ls = pl.ds(pl.multiple_of(lt * LANES, LANES), LANES)
        for a in range(et // N_KEYS):
            rows = pl.ds(a * N_KEYS, N_KEYS)
            acc = None
            for h in range(PEER_HEADS):
                keep = sb_ref[h, lt] >= thr_ref[h, lt, e, a:a + 1, :]
                gate = jnp.where(keep, eb_ref[h, lt], 0.0) * ea_ref[h, lt, e, a:a + 1, :]
                acc = gate if acc is None else acc + gate
            w_scr[rows, ls] = (acc * _gelu(h_scr[rows, ls])).astype(w_scr.dtype)
        return carry

    lax.fori_loop(0, tt // LANES, per_lane_tile, 0)
    o_ref[...] += lax.dot_general(w_scr[...], v_ref[...], _TN,
                                  preferred_element_type=jnp.float32)


def _peer_dense(hn, u, v, routing, tt, et):
    t, d = hn.shape
    n_exp = u.shape[0]
    return pl.pallas_call(
        _peer_kernel,
        grid=(t // tt, n_exp // et),
        in_specs=[pl.BlockSpec((tt, d), lambda i, e: (i, 0)),
                  pl.BlockSpec((et, d), lambda i, e: (e, 0)),
                  pl.BlockSpec((et, d), lambda i, e: (e, 0))] + _route_specs(tt, lambda i, e: i),
        out_specs=pl.BlockSpec((tt, d), lambda i, e: (i, 0)),
        out_shape=jax.ShapeDtypeStruct((t, d), jnp.float32),
        scratch_shapes=[pltpu.VMEM((et, tt), jnp.float32),
                        pltpu.VMEM((et, tt), jnp.bfloat16)],
        compiler_params=_params(),
        name="peer_dense",
    )(hn, u, v, *routing)


def _final_kernel(h_ref, p_ref, g_ref, o_ref):
    o_ref[...] = _rms(h_ref[...] + p_ref[...], g_ref[...])


def _final(h, p, g, tm):
    t, d = h.shape
    row = pl.BlockSpec((tm, d), lambda i: (i, 0))
    return pl.pallas_call(
        _final_kernel,
        grid=(t // tm,),
        in_specs=[row, row, pl.BlockSpec((1, d), lambda i: (0, 0))],
        out_specs=row,
        out_shape=jax.ShapeDtypeStruct((t, d), jnp.float32),
        compiler_params=_params(),
        name="final_norm",
    )(h, p, g.reshape(1, d))


def _tile(t, pref):
    while t % pref:
        pref //= 2
    return pref


def _project_in(x2, norm_g, w_in_bf):
    t = x2.shape[0]
    xn = _rmsnorm(x2, norm_g, jnp.bfloat16, _tile(t, 256))
    tm = _tile(t, 1024)
    cw = (w_in_bf.shape[1] - 3 * QKV_WIDTH)
    q = _matmul(xn, w_in_bf, 0, QKV_WIDTH, jnp.float32, tm, 1024)
    k = _matmul(xn, w_in_bf, QKV_WIDTH, QKV_WIDTH, jnp.float32, tm, 1024)
    v = _matmul(xn, w_in_bf, 2 * QKV_WIDTH, QKV_WIDTH, jnp.float32, tm, 1024)
    pc = _matmul(xn, w_in_bf, 3 * QKV_WIDTH, cw, jnp.float32, tm, 1024)
    return q, k, v, pc, xn


def _channel_mix(x2, att, cy, lw):
    t = x2.shape[0]
    cat = _catnorm(att, cy, lw["g_att"], lw["g_conv"], _tile(t, 256))
    h, hn = _outproj(cat, lw["w_out"], x2, lw["g_ffn"], _tile(t, 512))
    qp = _matmul(hn, lw["w_pq"], 0, lw["w_pq"].shape[1], jnp.bfloat16, _tile(t, 1024), 1024)
    routing = _route(qp, lw["ka"], lw["kb"], _tile(t, 512))
    peer = _peer_dense(hn, lw["u"], lw["v"], routing, _tile(t, 512), 8 * N_KEYS)
    return h, peer


def kernel(x_prompt, x_sample, cache_k_g0, cache_v_g0, cache_k_g1, cache_v_g1, cache_k_g2, cache_v_g2, state_conv, norm_mix, w_in, conv_w, norm_att_out, norm_conv_out, w_out, norm_ffn, w_pq, sub_keys_a, sub_keys_b, expert_u, expert_v, norm_final):
    bf = jnp.bfloat16
    depth = w_in.shape[0]
    b, s, d = x_prompt.shape
    bd, td, _ = x_sample.shape
    caches_k = (cache_k_g0, cache_k_g1, cache_k_g2)
    caches_v = (cache_v_g0, cache_v_g1, cache_v_g2)
    conv_c = conv_w.shape[2]

    hp = x_prompt.reshape(b * s, d)
    hs = x_sample.reshape(bd * td, d)
    nk_p = [[] for _ in range(N_GROUPS)]
    nv_p = [[] for _ in range(N_GROUPS)]
    nk_s = [[] for _ in range(N_GROUPS)]
    nv_s = [[] for _ in range(N_GROUPS)]
    nc_p, nc_s = [], []
    for l in range(depth):
        lw = dict(g_att=norm_att_out[l], g_conv=norm_conv_out[l], w_out=w_out[l].astype(bf),
                  g_ffn=norm_ffn[l], w_pq=w_pq[l].astype(bf), ka=sub_keys_a[l].astype(bf),
                  kb=sub_keys_b[l].astype(bf), u=expert_u[l].astype(bf), v=expert_v[l].astype(bf))
        w_in_bf = w_in[l].astype(bf)

        q, k, v, pc, xn = _project_in(hp, norm_mix[l], w_in_bf)
        att = _prompt_attention(q, k, v, b, s)
        cy, ns = _gated_conv(pc, jnp.zeros((b, CONV_K - 1, conv_c), jnp.float32), conv_w[l], b, s, 256)
        for g, (win, _) in enumerate(DIL_PATTERNS):
            rows = min(win, s)
            for dst, base in ((nk_p, QKV_WIDTH), (nv_p, 2 * QKV_WIDTH)):
                proj_t = _matmul_t(w_in_bf, xn, base + g * ATT_WIDTH, ATT_WIDTH, b, s, rows, 1024)
                dst[g].append(jnp.transpose(proj_t.reshape(b, ATT_SLOTS, HEAD_DIM, rows), (0, 3, 1, 2)))
        nc_p.append(ns)
        h_res, peer = _channel_mix(hp, att, cy, lw)
        last = l == depth - 1
        if last:
            y_prompt = _final(h_res, peer, norm_final, _tile(b * s, 256))
        else:
            hp = h_res + peer

        q, k, v, pc, _ = _project_in(hs, norm_mix[l], w_in_bf)
        new_t = lambda a: jnp.transpose(a.reshape(bd, td, N_GROUPS, ATT_SLOTS, HEAD_DIM), (0, 2, 3, 4, 1))
        ck_t = [jnp.transpose(c[l], (0, 2, 3, 1)) for c in caches_k]
        cv_t = [jnp.transpose(c[l], (0, 2, 3, 1)) for c in caches_v]
        att_t, rolled = _sample_attention(new_t(q), new_t(k), new_t(v), ck_t, cv_t)
        att = jnp.transpose(att_t, (0, 3, 1, 2)).reshape(bd * td, ATT_WIDTH)
        for g in range(N_GROUPS):
            nk_s[g].append(jnp.transpose(rolled[2 * g], (0, 3, 1, 2)))
            nv_s[g].append(jnp.transpose(rolled[2 * g + 1], (0, 3, 1, 2)))
        cy, ns = _gated_conv(pc, state_conv[l], conv_w[l], bd, td, conv_c)
        nc_s.append(ns)
        h_res, peer = _channel_mix(hs, att, cy, lw)
        if last:
            y_sample = _final(h_res, peer, norm_final, _tile(bd * td, 256))
        else:
            hs = h_res + peer

    return (y_prompt.reshape(b, s, d), y_sample.reshape(bd, td, d),
            jnp.stack(nk_p[0], 0), jnp.stack(nv_p[0], 0),
            jnp.stack(nk_p[1], 0), jnp.stack(nv_p[1], 0),
            jnp.stack(nk_p[2], 0), jnp.stack(nv_p[2], 0),
            jnp.stack(nc_p, 0),
            jnp.stack(nk_s[0], 0), jnp.stack(nv_s[0], 0),
            jnp.stack(nk_s[1], 0), jnp.stack(nv_s[1], 0),
            jnp.stack(nk_s[2], 0), jnp.stack(nv_s[2], 0),
            jnp.stack(nc_s, 0))
```

```python
import functools
import math

import jax
import jax.numpy as jnp
from jax import lax
from jax.experimental import pallas as pl
from jax.experimental.pallas import tpu as pltpu

HEAD_DIM = 64
ATT_SLOTS = 16
DIL_PATTERNS = ((128, 1), (512, 4), (2048, 16))
N_GROUPS = len(DIL_PATTERNS)
ATT_WIDTH = ATT_SLOTS * HEAD_DIM
QKV_WIDTH = N_GROUPS * ATT_WIDTH
CONV_K = 3
N_KEYS = 128
PEER_HEADS = 8
PEER_QDIM = 256
PEER_HALF = PEER_QDIM // 2
PEER_TOPK = 16
NORM_EPS = 1e-6
NEG_INF = -1e30

LANES = 128
HEADS_PER_SLAB = LANES // HEAD_DIM
N_SLABS = ATT_WIDTH // LANES
N_BACK = 128
VMEM_LIMIT = 56 * 1024 * 1024

_NT = (((1,), (1,)), ((), ()))
_TN = (((0,), (0,)), ((), ()))


def _params(vmem=None):
    return pltpu.CompilerParams(vmem_limit_bytes=vmem or VMEM_LIMIT)


def _rms(x, g):
    return x * lax.rsqrt(jnp.mean(x * x, axis=-1, keepdims=True) + NORM_EPS) * g


def _rmsnorm_kernel(x_ref, g_ref, o_ref):
    o_ref[...] = _rms(x_ref[...], g_ref[...]).astype(o_ref.dtype)


def _rmsnorm(x, g, out_dtype, tm):
    t, d = x.shape
    return pl.pallas_call(
        _rmsnorm_kernel,
        grid=(t // tm,),
        in_specs=[pl.BlockSpec((tm, d), lambda i: (i, 0)),
                  pl.BlockSpec((1, d), lambda i: (0, 0))],
        out_specs=pl.BlockSpec((tm, d), lambda i: (i, 0)),
        out_shape=jax.ShapeDtypeStruct((t, d), out_dtype),
        compiler_params=_params(),
        name="rmsnorm",
    )(x, g.reshape(1, d))


def _mm_kernel(a_ref, b_ref, o_ref):
    o_ref[...] = jnp.dot(a_ref[...], b_ref[...],
                         preferred_element_type=jnp.float32).astype(o_ref.dtype)


def _matmul(a, b, col_start, n_cols, out_dtype, tm, tn):
    t, k = a.shape
    off = col_start // tn
    return pl.pallas_call(
        _mm_kernel,
        grid=(t // tm, n_cols // tn),
        in_specs=[pl.BlockSpec((tm, k), lambda i, j: (i, 0)),
                  pl.BlockSpec((k, tn), lambda i, j: (0, j + off))],
        out_specs=pl.BlockSpec((tm, tn), lambda i, j: (i, j)),
        out_shape=jax.ShapeDtypeStruct((t, n_cols), out_dtype),
        compiler_params=_params(),
        name="matmul",
    )(a, b)


def _mm_t_kernel(w_ref, x_ref, o_ref):
    o_ref[...] = lax.dot_general(w_ref[...], x_ref[...], (((0,), (1,)), ((), ())),
                                 preferred_element_type=jnp.float32)


def _matmul_t(w, x, col_start, n_cols, n_seq, seq, rows, tn):
    k = x.shape[1]
    per = seq // rows
    off = col_start // tn
    return pl.pallas_call(
        _mm_t_kernel,
        grid=(n_seq, n_cols // tn),
        in_specs=[pl.BlockSpec((k, tn), lambda b, j: (0, j + off)),
                  pl.BlockSpec((rows, k), lambda b, j: (b * per + per - 1, 0))],
        out_specs=pl.BlockSpec((None, tn, rows), lambda b, j: (b, j, 0)),
        out_shape=jax.ShapeDtypeStruct((n_seq, n_cols, rows), jnp.float32),
        compiler_params=_params(),
        name="matmul_t",
    )(w, x)


def _conv_kernel(gb_ref, gc_ref, h_ref, st_ref, w_ref, y_ref, ns_ref):
    u = gc_ref[...] * h_ref[...]
    s = u.shape[0]
    row = lax.broadcasted_iota(jnp.int32, u.shape, 0)
    st = st_ref[...]
    w = w_ref[...]
    u1 = jnp.where(row == 0, st[1:2, :], pltpu.roll(u, 1, 0))
    u2 = jnp.where(row == 0, st[0:1, :],
                   jnp.where(row == 1, st[1:2, :], pltpu.roll(u, 2, 0)))
    y = u2 * w[0:1, :]
    y = y + u1 * w[1:2, :]
    y = y + u * w[2:3, :]
    y_ref[...] = gb_ref[...] * y
    ns_ref[...] = u[s - 2:s, :]


def _gated_conv(pc, state, conv_w, n_seq, seq, cb):
    c = conv_w.shape[1]
    ncb = c // cb
    return pl.pallas_call(
        _conv_kernel,
        grid=(n_seq, ncb),
        in_specs=[pl.BlockSpec((seq, cb), lambda b, j: (b, j)),
                  pl.BlockSpec((seq, cb), lambda b, j: (b, ncb + j)),
                  pl.BlockSpec((seq, cb), lambda b, j: (b, 2 * ncb + j)),
                  pl.BlockSpec((None, CONV_K - 1, cb), lambda b, j: (b, 0, j)),
                  pl.BlockSpec((CONV_K, cb), lambda b, j: (0, j))],
        out_specs=[pl.BlockSpec((seq, cb), lambda b, j: (b, j)),
                   pl.BlockSpec((None, CONV_K - 1, cb), lambda b, j: (b, 0, j))],
        out_shape=[jax.ShapeDtypeStruct((n_seq * seq, c), jnp.float32),
                   jax.ShapeDtypeStruct((n_seq, CONV_K - 1, c), jnp.float32)],
        compiler_params=_params(),
        name="gated_conv",
    )(pc, pc, pc, state, conv_w)


def _slope_like(shape, group, head):
    i = (group * ATT_SLOTS + 1 + head).astype(jnp.float32)
    n = float(N_GROUPS * ATT_SLOTS)
    return jnp.exp2(jnp.broadcast_to(-8.0 * i / n, shape))


def _merge_stats(stats):
    ms = [s[0] for s in stats]
    m = functools.reduce(jnp.maximum, ms)
    den = num = None
    for (mg, lg, ng) in stats:
        a = jnp.exp(mg - m)
        den = a * lg if den is None else den + a * lg
        num = a * ng if num is None else num + a * ng
    return num / den


def _prompt_attn_kernel(q0, k0, v0, q1, k1, v1, q2, k2, v2, o_ref, *stat_refs, seq):
    slab = pl.program_id(1)
    lane = lax.broadcasted_iota(jnp.int32, (1, LANES), 1)
    first = lane < HEAD_DIM
    scale = HEAD_DIM ** -0.5

    refs = ((q0, k0, v0), (q1, k1, v1), (q2, k2, v2))
    for g, (win, dil) in enumerate(DIL_PATTERNS):
        q_ref, k_ref, v_ref = refs[g]
        m_ref, l_ref, n_ref = stat_refs[3 * g:3 * g + 3]
        span = dil * N_BACK
        nb = seq // span
        two = nb > 1
        nk = (2 if two else 1) * N_BACK
        qi = lax.broadcasted_iota(jnp.int32, (N_BACK, nk), 0)
        kc = lax.broadcasted_iota(jnp.int32, (N_BACK, nk), 1)
        steps = (N_BACK if two else 0) + qi - kc
        in_band = (steps >= 0) & (steps <= N_BACK)
        dist = (steps * dil).astype(jnp.float32)
        slopes = [_slope_like((1, nk), g, slab * HEADS_PER_SLAB + h)
                  for h in range(HEADS_PER_SLAB)]

        def tile(t, carry, q_ref=q_ref, k_ref=k_ref, v_ref=v_ref, m_ref=m_ref, l_ref=l_ref,
                 n_ref=n_ref, dil=dil, span=span, nb=nb, two=two, kc=kc, in_band=in_band,
                 dist=dist, slopes=slopes):
            r = t // nb
            n = t - r * nb
            start = n * span + r
            rows = pl.ds(start, N_BACK, stride=dil)
            q = q_ref[rows, :] * scale
            if two:
                prev = jnp.maximum(start - span, r)
                prows = pl.ds(prev, N_BACK, stride=dil)
                kk = jnp.concatenate([k_ref[prows, :], k_ref[rows, :]], axis=0)
                vv = jnp.concatenate([v_ref[prows, :], v_ref[rows, :]], axis=0)
                valid = in_band & ((kc >= N_BACK) | (n > 0))
            else:
                kk, vv, valid = k_ref[rows, :], v_ref[rows, :], in_band
            kk = kk.astype(jnp.bfloat16)
            vv = vv.astype(jnp.bfloat16)
            stats = []
            for h in range(HEADS_PER_SLAB):
                mine = first if h == 0 else jnp.logical_not(first)
                qm = jnp.where(mine, q, 0.0).astype(jnp.bfloat16)
                s = lax.dot_general(qm, kk, _NT, preferred_element_type=jnp.float32)
                s = s + jnp.where(valid, -slopes[h] * dist, NEG_INF)
                m = jnp.max(s, axis=-1, keepdims=True)
                p = jnp.exp(s - m)
                l = jnp.sum(p, axis=-1, keepdims=True)
                o = jnp.dot(p.astype(jnp.bfloat16), vv, preferred_element_type=jnp.float32)
                stats.append((m, l, o))
            m_ref[rows, :] = jnp.where(first, stats[0][0], stats[1][0])
            l_ref[rows, :] = jnp.where(first, stats[0][1], stats[1][1])
            n_ref[rows, :] = jnp.where(first, stats[0][2], stats[1][2])
            return carry

        lax.fori_loop(0, dil * nb, tile, 0, unroll=8)

    chunk = 256

    def merge(c, carry):
        rows = pl.ds(pl.multiple_of(c * chunk, chunk), chunk)
        o_ref[rows, :] = _merge_stats([[stat_refs[3 * g + j][rows, :] for j in range(3)]
                                       for g in range(N_GROUPS)])
        return carry

    lax.fori_loop(0, seq // chunk, merge, 0)


def _prompt_attention(q, k, v, n_seq, seq):
    assert seq % (DIL_PATTERNS[-1][1] * N_BACK) == 0
    in_specs = []
    for g in range(N_GROUPS):
        for _ in range(3):
            in_specs.append(pl.BlockSpec((seq, LANES), lambda b, j, g=g: (b, g * N_SLABS + j)))
    args = []
    for g in range(N_GROUPS):
        args += [q, k, v]
    return pl.pallas_call(
        functools.partial(_prompt_attn_kernel, seq=seq),
        grid=(n_seq, N_SLABS),
        in_specs=in_specs,
        out_specs=pl.BlockSpec((seq, LANES), lambda b, j: (b, j)),
        out_shape=jax.ShapeDtypeStruct((n_seq * seq, ATT_WIDTH), jnp.float32),
        scratch_shapes=[pltpu.VMEM((seq, LANES), jnp.float32)] * (3 * N_GROUPS),
        compiler_params=_params(),
        name="prompt_attention",
    )(*args)


SAMPLE_HEADS_PER_STEP = 8


def _sample_kernel(*refs, t_new):
    qn_ref, kn_ref, vn_ref = refs[:3]
    cache_refs = refs[3:3 + 2 * N_GROUPS]
    out_refs = refs[3 + 2 * N_GROUPS:3 + 4 * N_GROUPS]
    o_ref = refs[3 + 4 * N_GROUPS]
    hb = o_ref.shape[0]
    scale = HEAD_DIM ** -0.5
    lane = lax.broadcasted_iota(jnp.int32, (1, LANES), 1)
    trow = lax.broadcasted_iota(jnp.int32, (t_new, 1), 0)

    def widen(x):
        return jnp.concatenate([x, jnp.zeros((x.shape[0], LANES - t_new), x.dtype)], axis=1)

    def per_head(h, carry):
        head = pl.program_id(1) * hb + h
        score_tiles = []
        for g, (win, dil) in enumerate(DIL_PATTERNS):
            ck = cache_refs[2 * g]
            n_tiles = win // LANES
            per_pass = min(dil, t_new)
            q_t = qn_ref[g, h] * scale
            k_new = widen(kn_ref[g, h])
            slope = _slope_like((1, LANES), g, head)
            res = lane & (dil - 1)
            tiles = [jnp.zeros((t_new, LANES), jnp.float32) for _ in range(n_tiles + 1)]
            for k in range(t_new // per_pass):
                q_mix = None
                for t in range(k * per_pass, (k + 1) * per_pass):
                    q_col = jnp.broadcast_to(q_t[:, t:t + 1], (HEAD_DIM, LANES))
                    row = jnp.sum(k_new * q_col, axis=0, keepdims=True)
                    tiles[n_tiles] = jnp.where(trow == t, row, tiles[n_tiles])
                    term = q_col if per_pass == 1 else jnp.where(res == t % dil, q_col, 0.0)
                    q_mix = term if q_mix is None else q_mix + term
                in_pass = (trow >= k * per_pass) & (trow < (k + 1) * per_pass)
                for j in range(n_tiles):
                    row = jnp.sum(ck[h, :, j * LANES:(j + 1) * LANES] * q_mix, axis=0, keepdims=True)
                    tiles[j] = jnp.where(in_pass, row, tiles[j])
            for j in range(n_tiles + 1):
                if j < n_tiles:
                    dist = win + trow - (j * LANES + lane)
                    valid = ((dist & (dil - 1)) == 0) & (dist <= win)
                else:
                    dist = trow - lane
                    valid = ((dist & (dil - 1)) == 0) & (dist >= 0)
                tiles[j] = jnp.where(valid, tiles[j] - slope * dist.astype(jnp.float32), NEG_INF)
            score_tiles.append(tiles)
        flat = [s for tiles in score_tiles for s in tiles]
        m = functools.reduce(jnp.maximum, [jnp.max(s, axis=1, keepdims=True) for s in flat])
        denom = functools.reduce(lambda a, b: a + b,
                                 [jnp.sum(jnp.exp(s - m), axis=1, keepdims=True) for s in flat])
        probs = [[jnp.exp(s - m) / denom for s in tiles] for tiles in score_tiles]
        pass_sums = {}
        for g, (win, dil) in enumerate(DIL_PATTERNS):
            cv = cache_refs[2 * g + 1]
            per_pass = min(dil, t_new)
            for k in range(t_new // per_pass):
                in_pass = (trow >= k * per_pass) & (trow < (k + 1) * per_pass)
                acc = None
                for j in range(win // LANES):
                    w_row = jnp.sum(jnp.where(in_pass, probs[g][j], 0.0), axis=0, keepdims=True)
                    term = cv[h, :, j * LANES:(j + 1) * LANES] * w_row
                    acc = term if acc is None else acc + term
                pass_sums[g, k] = acc
        out = jnp.zeros((HEAD_DIM, LANES), jnp.float32)
        for t in range(t_new):
            tot = None
            for g, (win, dil) in enumerate(DIL_PATTERNS):
                per_pass = min(dil, t_new)
                mine = pass_sums[g, t // per_pass]
                if per_pass > 1:
                    mine = jnp.where((lane & (dil - 1)) == t % dil, mine, 0.0)
                mine = mine + widen(vn_ref[g, h]) * probs[g][win // LANES][t:t + 1, :]
                tot = mine if tot is None else tot + mine
            out = jnp.where(lane == t, jnp.sum(tot, axis=1, keepdims=True), out)
        o_ref[h] = out[:, 0:t_new]
        for g, (win, dil) in enumerate(DIL_PATTERNS):
            for src, new, dst in ((cache_refs[2 * g], kn_ref, out_refs[2 * g]),
                                  (cache_refs[2 * g + 1], vn_ref, out_refs[2 * g + 1])):
                rolled = pltpu.roll(src[h], win - t_new, 1)
                top = pltpu.roll(widen(new[g, h]), LANES - t_new, 1)
                if win > LANES:
                    dst[h, :, 0:win - LANES] = rolled[:, 0:win - LANES]
                dst[h, :, win - LANES:win] = jnp.where(lane >= LANES - t_new, top,
                                                       rolled[:, win - LANES:win])
        return carry

    lax.fori_loop(0, hb, per_head, 0)


def _sample_attention(q_new, k_new, v_new, caches_k, caches_v):
    n, _, nh, dh, t_new = q_new.shape
    hb = SAMPLE_HEADS_PER_STEP
    new_spec = pl.BlockSpec((None, N_GROUPS, hb, dh, t_new), lambda b, j: (b, 0, j, 0, 0))
    in_specs, args, out_specs, out_shape = [new_spec] * 3, [q_new, k_new, v_new], [], []
    for g, (win, dil) in enumerate(DIL_PATTERNS):
        assert caches_k[g].shape == (n, nh, dh, win) and win % LANES == 0 and dil & (dil - 1) == 0
        spec = pl.BlockSpec((None, hb, dh, win), lambda b, j: (b, j, 0, 0))
        in_specs += [spec, spec]
        args += [caches_k[g], caches_v[g]]
        out_specs += [spec, spec]
        out_shape += [jax.ShapeDtypeStruct(caches_k[g].shape, jnp.float32)] * 2
    out_specs.append(pl.BlockSpec((None, hb, dh, t_new), lambda b, j: (b, j, 0, 0)))
    out_shape.append(jax.ShapeDtypeStruct((n, nh, dh, t_new), jnp.float32))
    res = pl.pallas_call(
        functools.partial(_sample_kernel, t_new=t_new),
        grid=(n, nh // hb),
        in_specs=in_specs,
        out_specs=out_specs,
        out_shape=out_shape,
        compiler_params=_params(),
        name="sample_attention",
    )(*args)
    return res[-1], res[:-1]


def _catnorm_kernel(att_ref, cy_ref, ga_ref, gc_ref, o_ref):
    wa = att_ref.shape[1]
    o_ref[:, 0:wa] = _rms(att_ref[...], ga_ref[...]).astype(o_ref.dtype)
    o_ref[:, wa:] = _rms(cy_ref[...], gc_ref[...]).astype(o_ref.dtype)


def _catnorm(att, cy, g_att, g_conv, tm):
    t, wa = att.shape
    wc = cy.shape[1]
    return pl.pallas_call(
        _catnorm_kernel,
        grid=(t // tm,),
        in_specs=[pl.BlockSpec((tm, wa), lambda i: (i, 0)),
                  pl.BlockSpec((tm, wc), lambda i: (i, 0)),
                  pl.BlockSpec((1, wa), lambda i: (0, 0)),
                  pl.BlockSpec((1, wc), lambda i: (0, 0))],
        out_specs=pl.BlockSpec((tm, wa + wc), lambda i: (i, 0)),
        out_shape=jax.ShapeDtypeStruct((t, wa + wc), jnp.bfloat16),
        compiler_params=_params(),
        name="catnorm",
    )(att, cy, g_att.reshape(1, wa), g_conv.reshape(1, wc))


def _outproj_kernel(a_ref, w_ref, x_ref, g_ref, h_ref, hn_ref):
    h = x_ref[...] + jnp.dot(a_ref[...], w_ref[...], preferred_element_type=jnp.float32)
    h_ref[...] = h
    hn_ref[...] = _rms(h, g_ref[...]).astype(hn_ref.dtype)


def _outproj(cat, w_out, x, g_ffn, tm):
    t, k = cat.shape
    d = w_out.shape[1]
    return pl.pallas_call(
        _outproj_kernel,
        grid=(t // tm,),
        in_specs=[pl.BlockSpec((tm, k), lambda i: (i, 0)),
                  pl.BlockSpec((k, d), lambda i: (0, 0)),
                  pl.BlockSpec((tm, d), lambda i: (i, 0)),
                  pl.BlockSpec((1, d), lambda i: (0, 0))],
        out_specs=[pl.BlockSpec((tm, d), lambda i: (i, 0)),
                   pl.BlockSpec((tm, d), lambda i: (i, 0))],
        out_shape=[jax.ShapeDtypeStruct((t, d), jnp.float32),
                   jax.ShapeDtypeStruct((t, d), jnp.bfloat16)],
        compiler_params=_params(),
        name="outproj",
    )(cat, w_out, x, g_ffn.reshape(1, d))


_N_RANK = PEER_TOPK + 1
_CAND_PER_RANK = tuple(_N_RANK // (i + 1) for i in range(_N_RANK))
_N_CAND = sum(_CAND_PER_RANK)
_CAND_ROWS = 8 << (-(-_N_CAND // 8) - 1).bit_length()


def _merge_desc(xs):
    n = len(xs)
    if n == 1:
        return xs
    half = n // 2
    hi = [jnp.maximum(xs[i], xs[i + half]) for i in range(half)]
    lo = [jnp.minimum(xs[i], xs[i + half]) for i in range(half)]
    return _merge_desc(hi) + _merge_desc(lo)


def _sort_desc(xs):
    n = len(xs)
    if n == 1:
        return xs
    return _merge_desc(_sort_desc(xs[:n // 2]) + _sort_desc(xs[n // 2:])[::-1])


def _top_sorted(x, k):
    rows, tt = x.shape
    keep = 32
    n = rows // 8
    assert rows == 8 * n and n & (n - 1) == 0 and k <= min(keep, rows)
    outs = []
    for lt in range(tt // LANES):
        xs = _sort_desc([x[8 * i:8 * i + 8, lt * LANES:(lt + 1) * LANES] for i in range(n)])
        for shift in (4, 2, 1):
            other = [pltpu.roll(v, shift, 0) for v in xs]
            m = len(xs)
            if 2 * m <= keep:
                xs = _merge_desc(xs + other[::-1])
            else:
                xs = _merge_desc([jnp.maximum(xs[i], other[m - 1 - i]) for i in range(m)])
        outs.append(jnp.concatenate([v[0:1, :] for v in xs[:k]], axis=0))
    return jnp.concatenate(outs, axis=1)


def _route_kernel(q_ref, ka_ref, kb_ref, thr_ref, sb_ref, ea_ref, eb_ref):
    ka = ka_ref[...]
    kb = kb_ref[...]
    tt = q_ref.shape[0]
    sub = N_KEYS // 8

    def per_head(h, carry):
        c0 = pl.multiple_of(h * PEER_QDIM, PEER_QDIM)
        qa = q_ref[:, pl.ds(c0, PEER_HALF)]
        qb = q_ref[:, pl.ds(c0 + PEER_HALF, PEER_HALF)]
        sa = lax.dot_general(ka, qa, _NT, preferred_element_type=jnp.float32)
        sb = lax.dot_general(kb, qb, _NT, preferred_element_type=jnp.float32)
        va = _top_sorted(sa, _N_RANK)
        vb = _top_sorted(sb, _N_RANK)
        pieces = [va[i:i + 1, :] + vb[0:n, :] for i, n in enumerate(_CAND_PER_RANK)]
        pieces.append(jnp.full((_CAND_ROWS - _N_CAND, tt), -jnp.inf, jnp.float32))
        top = _top_sorted(jnp.concatenate(pieces, axis=0), _N_RANK)
        z = jnp.sum(jnp.exp(top[0:PEER_TOPK, :] - top[0:1, :]), axis=0, keepdims=True)
        tau = 0.5 * (top[PEER_TOPK - 1:PEER_TOPK, :] + top[PEER_TOPK:PEER_TOPK + 1, :])
        thr = tau - sa
        ea = jnp.exp(sa - va[0:1, :])
        eb = jnp.exp(sb - vb[0:1, :]) / z
        for lt in range(tt // LANES):
            ls = slice(lt * LANES, (lt + 1) * LANES)
            thr_ref[h, lt] = thr[:, ls].reshape(sub, 8, LANES)
            ea_ref[h, lt] = ea[:, ls].reshape(sub, 8, LANES)
            sb_ref[h, lt] = sb[:, ls]
            eb_ref[h, lt] = eb[:, ls]
        return carry

    lax.fori_loop(0, PEER_HEADS, per_head, 0)


def _route_specs(tt, imap):
    n_lt = tt // LANES
    row = pl.BlockSpec((PEER_HEADS, n_lt, N_KEYS // 8, 8, LANES), lambda *g: (0, imap(*g), 0, 0, 0))
    full = pl.BlockSpec((PEER_HEADS, n_lt, N_KEYS, LANES), lambda *g: (0, imap(*g), 0, 0))
    return [row, full, row, full]


def _route(qp, ka, kb, tt):
    t = qp.shape[0]
    row_shape = jax.ShapeDtypeStruct((PEER_HEADS, t // LANES, N_KEYS // 8, 8, LANES), jnp.float32)
    full_shape = jax.ShapeDtypeStruct((PEER_HEADS, t // LANES, N_KEYS, LANES), jnp.float32)
    return pl.pallas_call(
        _route_kernel,
        grid=(t // tt,),
        in_specs=[pl.BlockSpec((tt, PEER_HEADS * PEER_QDIM), lambda i: (i, 0)),
                  pl.BlockSpec((N_KEYS, PEER_HALF), lambda i: (0, 0)),
                  pl.BlockSpec((N_KEYS, PEER_HALF), lambda i: (0, 0))],
        out_specs=_route_specs(tt, lambda i: i),
        out_shape=[row_shape, full_shape, row_shape, full_shape],
        compiler_params=_params(),
        name="peer_route",
    )(qp, ka, kb)


def _gelu(x):
    return 0.5 * x * (1.0 + lax.erf(x * math.sqrt(0.5)))


def _peer_kernel(hn_ref, u_ref, v_ref, thr_ref, sb_ref, ea_ref, eb_ref, o_ref, h_scr, w_scr):
    e = pl.program_id(1)
    et, tt = h_scr.shape
    assert et == 8 * N_KEYS

    @pl.when(e == 0)
    def _():
        o_ref[...] = jnp.zeros(o_ref.shape, o_ref.dtype)

    h_scr[...] = lax.dot_general(u_ref[...], hn_ref[...], _NT,
                                 preferred_element_type=jnp.float32)

    def per_lane_tile(lt, carry):
        ls = pl.ds(pl.multiple_of(lt * LANES, LANES), LANES)
        for a in range(et // N_KEYS):
            rows = pl.ds(a * N_KEYS, N_KEYS)
            acc = None
            for h in range(PEER_HEADS):
                keep = sb_ref[h, lt] >= thr_ref[h, lt, e, a:a + 1, :]
                gate = jnp.where(keep, eb_ref[h, lt], 0.0) * ea_ref[h, lt, e, a:a + 1, :]
                acc = gate if acc is None else acc + gate
            w_scr[rows, ls] = (acc * _gelu(h_scr[rows, ls])).astype(w_scr.dtype)
        return carry

    lax.fori_loop(0, tt // LANES, per_lane_tile, 0)
    o_ref[...] += lax.dot_general(w_scr[...], v_ref[...], _TN,
                                  preferred_element_type=jnp.float32)


def _peer_dense(hn, u, v, routing, tt, et):
    t, d = hn.shape
    n_exp = u.shape[0]
    return pl.pallas_call(
        _peer_kernel,
        grid=(t // tt, n_exp // et),
        in_specs=[pl.BlockSpec((tt, d), lambda i, e: (i, 0)),
                  pl.BlockSpec((et, d), lambda i, e: (e, 0)),
                  pl.BlockSpec((et, d), lambda i, e: (e, 0))] + _route_specs(tt, lambda i, e: i),
        out_specs=pl.BlockSpec((tt, d), lambda i, e: (i, 0)),
        out_shape=jax.ShapeDtypeStruct((t, d), jnp.float32),
        scratch_shapes=[pltpu.VMEM((et, tt), jnp.float32),
                        pltpu.VMEM((et, tt), jnp.bfloat16)],
        compiler_params=_params(),
        name="peer_dense",
    )(hn, u, v, *routing)


def _final_kernel(h_ref, p_ref, g_ref, o_ref):
    o_ref[...] = _rms(h_ref[...] + p_ref[...], g_ref[...])


def _final(h, p, g, tm):
    t, d = h.shape
    row = pl.BlockSpec((tm, d), lambda i: (i, 0))
    return pl.pallas_call(
        _final_kernel,
        grid=(t // tm,),
        in_specs=[row, row, pl.BlockSpec((1, d), lambda i: (0, 0))],
        out_specs=row,
        out_shape=jax.ShapeDtypeStruct((t, d), jnp.float32),
        compiler_params=_params(),
        name="final_norm",
    )(h, p, g.reshape(1, d))


def _tile(t, pref):
    while t % pref:
        pref //= 2
    return pref


def _project_in(x2, norm_g, w_in_bf):
    t = x2.shape[0]
    xn = _rmsnorm(x2, norm_g, jnp.bfloat16, _tile(t, 256))
    tm = _tile(t, 1024)
    cw = (w_in_bf.shape[1] - 3 * QKV_WIDTH)
    q = _matmul(xn, w_in_bf, 0, QKV_WIDTH, jnp.float32, tm, 1024)
    k = _matmul(xn, w_in_bf, QKV_WIDTH, QKV_WIDTH, jnp.float32, tm, 1024)
    v = _matmul(xn, w_in_bf, 2 * QKV_WIDTH, QKV_WIDTH, jnp.float32, tm, 1024)
    pc = _matmul(xn, w_in_bf, 3 * QKV_WIDTH, cw, jnp.float32, tm, 1024)
    return q, k, v, pc, xn


def _channel_mix(x2, att, cy, lw):
    t = x2.shape[0]
    cat = _catnorm(att, cy, lw["g_att"], lw["g_conv"], _tile(t, 256))
    h, hn = _outproj(cat, lw["w_out"], x2, lw["g_ffn"], _tile(t, 512))
    qp = _matmul(hn, lw["w_pq"], 0, lw["w_pq"].shape[1], jnp.bfloat16, _tile(t, 1024), 1024)
    routing = _route(qp, lw["ka"], lw["kb"], _tile(t, 512))
    peer = _peer_dense(hn, lw["u"], lw["v"], routing, _tile(t, 512), 8 * N_KEYS)
    return h, peer


def kernel(x_prompt, x_sample, cache_k_g0, cache_v_g0, cache_k_g1, cache_v_g1, cache_k_g2, cache_v_g2, state_conv, norm_mix, w_in, conv_w, norm_att_out, norm_conv_out, w_out, norm_ffn, w_pq, sub_keys_a, sub_keys_b, expert_u, expert_v, norm_final):
    bf = jnp.bfloat16
    depth = w_in.shape[0]
    b, s, d = x_prompt.shape
    bd, td, _ = x_sample.shape
    caches_k = (cache_k_g0, cache_k_g1, cache_k_g2)
    caches_v = (cache_v_g0, cache_v_g1, cache_v_g2)
    conv_c = conv_w.shape[2]

    hp = x_prompt.reshape(b * s, d)
    hs = x_sample.reshape(bd * td, d)
    nk_p = [[] for _ in range(N_GROUPS)]
    nv_p = [[] for _ in range(N_GROUPS)]
    nk_s = [[] for _ in range(N_GROUPS)]
    nv_s = [[] for _ in range(N_GROUPS)]
    nc_p, nc_s = [], []
    for l in range(depth):
        lw = dict(g_att=norm_att_out[l], g_conv=norm_conv_out[l], w_out=w_out[l].astype(bf),
                  g_ffn=norm_ffn[l], w_pq=w_pq[l].astype(bf), ka=sub_keys_a[l].astype(bf),
                  kb=sub_keys_b[l].astype(bf), u=expert_u[l].astype(bf), v=expert_v[l].astype(bf))
        w_in_bf = w_in[l].astype(bf)

        q, k, v, pc, xn = _project_in(hp, norm_mix[l], w_in_bf)
        att = _prompt_attention(q, k, v, b, s)
        cy, ns = _gated_conv(pc, jnp.zeros((b, CONV_K - 1, conv_c), jnp.float32), conv_w[l], b, s, 256)
        for g, (win, _) in enumerate(DIL_PATTERNS):
            rows = min(win, s)
            for dst, base in ((nk_p, QKV_WIDTH), (nv_p, 2 * QKV_WIDTH)):
                proj_t = _matmul_t(w_in_bf, xn, base + g * ATT_WIDTH, ATT_WIDTH, b, s, rows, 1024)
                dst[g].append(jnp.transpose(proj_t.reshape(b, ATT_SLOTS, HEAD_DIM, rows), (0, 3, 1, 2)))
        nc_p.append(ns)
        h_res, peer = _channel_mix(hp, att, cy, lw)
        last = l == depth - 1
        if last:
            y_prompt = _final(h_res, peer, norm_final, _tile(b * s, 256))
        else:
            hp = h_res + peer

        q, k, v, pc, _ = _project_in(hs, norm_mix[l], w_in_bf)
        new_t = lambda a: jnp.transpose(a.reshape(bd, td, N_GROUPS, ATT_SLOTS, HEAD_DIM), (0, 2, 3, 4, 1))
        ck_t = [jnp.transpose(c[l], (0, 2, 3, 1)) for c in caches_k]
        cv_t = [jnp.transpose(c[l], (0, 2, 3, 1)) for c in caches_v]
        att_t, rolled = _sample_attention(new_t(q), new_t(k), new_t(v), ck_t, cv_t)
        att = jnp.transpose(att_t, (0, 3, 1, 2)).reshape(bd * td, ATT_WIDTH)
        for g in range(N_GROUPS):
            nk_s[g].append(jnp.transpose(rolled[2 * g], (0, 3, 1, 2)))
            nv_s[g].append(jnp.transpose(rolled[2 * g + 1], (0, 3, 1, 2)))
        cy, ns = _gated_conv(pc, state_conv[l], conv_w[l], bd, td, conv_c)
        nc_s.append(ns)
        h_res, peer = _channel_mix(hs, att, cy, lw)
        if last:
            y_sample = _final(h_res, peer, norm_final, _tile(bd * td, 256))
        else:
            hs = h_res + peer

    return (y_prompt.reshape(b, s, d), y_sample.reshape(bd, td, d),
            jnp.stack(nk_p[0], 0), jnp.stack(nv_p[0], 0),
            jnp.stack(nk_p[1], 0), jnp.stack(nv_p[1], 0),
            jnp.stack(nk_p[2], 0), jnp.stack(nv_p[2], 0),
            jnp.stack(nc_p, 0),
            jnp.stack(nk_s[0], 0), jnp.stack(nv_s[0], 0),
            jnp.stack(nk_s[1], 0), jnp.stack(nv_s[1], 0),
            jnp.stack(nk_s[2], 0), jnp.stack(nv_s[2], 0),
            jnp.stack(nc_s, 0))
```

```python
import functools
import math

import jax
import jax.numpy as jnp
from jax import lax
from jax.experimental import pallas as pl
from jax.experimental.pallas import tpu as pltpu

HEAD_DIM = 64
ATT_SLOTS = 16
DIL_PATTERNS = ((128, 1), (512, 4), (2048, 16))
N_GROUPS = len(DIL_PATTERNS)
ATT_WIDTH = ATT_SLOTS * HEAD_DIM
QKV_WIDTH = N_GROUPS * ATT_WIDTH
CONV_K = 3
N_KEYS = 128
PEER_HEADS = 8
PEER_QDIM = 256
PEER_HALF = PEER_QDIM // 2
PEER_TOPK = 16
NORM_EPS = 1e-6
NEG_INF = -1e30

LANES = 128
HEADS_PER_SLAB = LANES // HEAD_DIM
N_SLABS = ATT_WIDTH // LANES
N_BACK = 128
VMEM_LIMIT = 56 * 1024 * 1024

_NT = (((1,), (1,)), ((), ()))
_TN = (((0,), (0,)), ((), ()))


def _params(vmem=None):
    return pltpu.CompilerParams(vmem_limit_bytes=vmem or VMEM_LIMIT)


def _rms(x, g):
    return x * lax.rsqrt(jnp.mean(x * x, axis=-1, keepdims=True) + NORM_EPS) * g


def _rmsnorm_kernel(x_ref, g_ref, o_ref):
    o_ref[...] = _rms(x_ref[...], g_ref[...]).astype(o_ref.dtype)


def _rmsnorm(x, g, out_dtype, tm):
    t, d = x.shape
    return pl.pallas_call(
        _rmsnorm_kernel,
        grid=(t // tm,),
        in_specs=[pl.BlockSpec((tm, d), lambda i: (i, 0)),
                  pl.BlockSpec((1, d), lambda i: (0, 0))],
        out_specs=pl.BlockSpec((tm, d), lambda i: (i, 0)),
        out_shape=jax.ShapeDtypeStruct((t, d), out_dtype),
        compiler_params=_params(),
        name="rmsnorm",
    )(x, g.reshape(1, d))


def _mm_kernel(a_ref, b_ref, o_ref):
    o_ref[...] = jnp.dot(a_ref[...], b_ref[...],
                         preferred_element_type=jnp.float32).astype(o_ref.dtype)


def _matmul(a, b, col_start, n_cols, out_dtype, tm, tn):
    t, k = a.shape
    off = col_start // tn
    return pl.pallas_call(
        _mm_kernel,
        grid=(t // tm, n_cols // tn),
        in_specs=[pl.BlockSpec((tm, k), lambda i, j: (i, 0)),
                  pl.BlockSpec((k, tn), lambda i, j: (0, j + off))],
        out_specs=pl.BlockSpec((tm, tn), lambda i, j: (i, j)),
        out_shape=jax.ShapeDtypeStruct((t, n_cols), out_dtype),
        compiler_params=_params(),
        name="matmul",
    )(a, b)


def _mm_t_kernel(w_ref, x_ref, o_ref):
    o_ref[...] = lax.dot_general(w_ref[...], x_ref[...], (((0,), (1,)), ((), ())),
                                 preferred_element_type=jnp.float32)


def _matmul_t(w, x, col_start, n_cols, n_seq, seq, rows, tn):
    k = x.shape[1]
    per = seq // rows
    off = col_start // tn
    return pl.pallas_call(
        _mm_t_kernel,
        grid=(n_seq, n_cols // tn),
        in_specs=[pl.BlockSpec((k, tn), lambda b, j: (0, j + off)),
                  pl.BlockSpec((rows, k), lambda b, j: (b * per + per - 1, 0))],
        out_specs=pl.BlockSpec((None, tn, rows), lambda b, j: (b, j, 0)),
        out_shape=jax.ShapeDtypeStruct((n_seq, n_cols, rows), jnp.float32),
        compiler_params=_params(),
        name="matmul_t",
    )(w, x)


def _conv_kernel(gb_ref, gc_ref, h_ref, st_ref, w_ref, y_ref, ns_ref):
    u = gc_ref[...] * h_ref[...]
    s = u.shape[0]
    row = lax.broadcasted_iota(jnp.int32, u.shape, 0)
    st = st_ref[...]
    w = w_ref[...]
    u1 = jnp.where(row == 0, st[1:2, :], pltpu.roll(u, 1, 0))
    u2 = jnp.where(row == 0, st[0:1, :],
                   jnp.where(row == 1, st[1:2, :], pltpu.roll(u, 2, 0)))
    y = u2 * w[0:1, :]
    y = y + u1 * w[1:2, :]
    y = y + u * w[2:3, :]
    y_ref[...] = gb_ref[...] * y
    ns_ref[...] = u[s - 2:s, :]


def _gated_conv(pc, state, conv_w, n_seq, seq, cb):
    c = conv_w.shape[1]
    ncb = c // cb
    return pl.pallas_call(
        _conv_kernel,
        grid=(n_seq, ncb),
        in_specs=[pl.BlockSpec((seq, cb), lambda b, j: (b, j)),
                  pl.BlockSpec((seq, cb), lambda b, j: (b, ncb + j)),
                  pl.BlockSpec((seq, cb), lambda b, j: (b, 2 * ncb + j)),
                  pl.BlockSpec((None, CONV_K - 1, cb), lambda b, j: (b, 0, j)),
                  pl.BlockSpec((CONV_K, cb), lambda b, j: (0, j))],
        out_specs=[pl.BlockSpec((seq, cb), lambda b, j: (b, j)),
                   pl.BlockSpec((None, CONV_K - 1, cb), lambda b, j: (b, 0, j))],
        out_shape=[jax.ShapeDtypeStruct((n_seq * seq, c), jnp.float32),
                   jax.ShapeDtypeStruct((n_seq, CONV_K - 1, c), jnp.float32)],
        compiler_params=_params(),
        name="gated_conv",
    )(pc, pc, pc, state, conv_w)


def _slope_like(shape, group, head):
    i = (group * ATT_SLOTS + 1 + head).astype(jnp.float32)
    n = float(N_GROUPS * ATT_SLOTS)
    return jnp.exp2(jnp.broadcast_to(-8.0 * i / n, shape))


def _merge_stats(stats):
    ms = [s[0] for s in stats]
    m = functools.reduce(jnp.maximum, ms)
    den = num = None
    for (mg, lg, ng) in stats:
        a = jnp.exp(mg - m)
        den = a * lg if den is None else den + a * lg
        num = a * ng if num is None else num + a * ng
    return num / den


def _prompt_attn_kernel(q0, k0, v0, q1, k1, v1, q2, k2, v2, o_ref, *stat_refs, seq):
    slab = pl.program_id(1)
    lane = lax.broadcasted_iota(jnp.int32, (1, LANES), 1)
    first = lane < HEAD_DIM
    scale = HEAD_DIM ** -0.5

    refs = ((q0, k0, v0), (q1, k1, v1), (q2, k2, v2))
    for g, (win, dil) in enumerate(DIL_PATTERNS):
        q_ref, k_ref, v_ref = refs[g]
        m_ref, l_ref, n_ref = stat_refs[3 * g:3 * g + 3]
        span = dil * N_BACK
        nb = seq // span
        two = nb > 1
        nk = (2 if two else 1) * N_BACK
        qi = lax.broadcasted_iota(jnp.int32, (N_BACK, nk), 0)
        kc = lax.broadcasted_iota(jnp.int32, (N_BACK, nk), 1)
        steps = (N_BACK if two else 0) + qi - kc
        in_band = (steps >= 0) & (steps <= N_BACK)
        dist = (steps * dil).astype(jnp.float32)
        slopes = [_slope_like((1, nk), g, slab * HEADS_PER_SLAB + h)
                  for h in range(HEADS_PER_SLAB)]

        def tile(t, carry, q_ref=q_ref, k_ref=k_ref, v_ref=v_ref, m_ref=m_ref, l_ref=l_ref,
                 n_ref=n_ref, dil=dil, span=span, nb=nb, two=two, kc=kc, in_band=in_band,
                 dist=dist, slopes=slopes):
            r = t // nb
            n = t - r * nb
            start = n * span + r
            rows = pl.ds(start, N_BACK, stride=dil)
            q = q_ref[rows, :] * scale
            if two:
                prev = jnp.maximum(start - span, r)
                prows = pl.ds(prev, N_BACK, stride=dil)
                kk = jnp.concatenate([k_ref[prows, :], k_ref[rows, :]], axis=0)
                vv = jnp.concatenate([v_ref[prows, :], v_ref[rows, :]], axis=0)
                valid = in_band & ((kc >= N_BACK) | (n > 0))
            else:
                kk, vv, valid = k_ref[rows, :], v_ref[rows, :], in_band
            kk = kk.astype(jnp.bfloat16)
            vv = vv.astype(jnp.bfloat16)
            stats = []
            for h in range(HEADS_PER_SLAB):
                mine = first if h == 0 else jnp.logical_not(first)
                qm = jnp.where(mine, q, 0.0).astype(jnp.bfloat16)
                s = lax.dot_general(qm, kk, _NT, preferred_element_type=jnp.float32)
                s = s + jnp.where(valid, -slopes[h] * dist, NEG_INF)
                m = jnp.max(s, axis=-1, keepdims=True)
                p = jnp.exp(s - m)
                l = jnp.sum(p, axis=-1, keepdims=True)
                o = jnp.dot(p.astype(jnp.bfloat16), vv, preferred_element_type=jnp.float32)
                stats.append((m, l, o))
            m_ref[rows, :] = jnp.where(first, stats[0][0], stats[1][0])
            l_ref[rows, :] = jnp.where(first, stats[0][1], stats[1][1])
            n_ref[rows, :] = jnp.where(first, stats[0][2], stats[1][2])
            return carry

        lax.fori_loop(0, dil * nb, tile, 0, unroll=8)

    chunk = 256

    def merge(c, carry):
        rows = pl.ds(pl.multiple_of(c * chunk, chunk), chunk)
        o_ref[rows, :] = _merge_stats([[stat_refs[3 * g + j][rows, :] for j in range(3)]
                                       for g in range(N_GROUPS)])
        return carry

    lax.fori_loop(0, seq // chunk, merge, 0)


def _prompt_attention(q, k, v, n_seq, seq):
    assert seq % (DIL_PATTERNS[-1][1] * N_BACK) == 0
    in_specs = []
    for g in range(N_GROUPS):
        for _ in range(3):
            in_specs.append(pl.BlockSpec((seq, LANES), lambda b, j, g=g: (b, g * N_SLABS + j)))
    args = []
    for g in range(N_GROUPS):
        args += [q, k, v]
    return pl.pallas_call(
        functools.partial(_prompt_attn_kernel, seq=seq),
        grid=(n_seq, N_SLABS),
        in_specs=in_specs,
        out_specs=pl.BlockSpec((seq, LANES), lambda b, j: (b, j)),
        out_shape=jax.ShapeDtypeStruct((n_seq * seq, ATT_WIDTH), jnp.float32),
        scratch_shapes=[pltpu.VMEM((seq, LANES), jnp.float32)] * (3 * N_GROUPS),
        compiler_params=_params(),
        name="prompt_attention",
    )(*args)


SAMPLE_HEADS_PER_STEP = 8


def _sample_kernel(*refs, t_new):
    qn_ref, kn_ref, vn_ref = refs[:3]
    cache_refs = refs[3:3 + 2 * N_GROUPS]
    out_refs = refs[3 + 2 * N_GROUPS:3 + 4 * N_GROUPS]
    o_ref = refs[3 + 4 * N_GROUPS]
    hb = o_ref.shape[0]
    scale = HEAD_DIM ** -0.5
    lane = lax.broadcasted_iota(jnp.int32, (1, LANES), 1)
    trow = lax.broadcasted_iota(jnp.int32, (t_new, 1), 0)

    def widen(x):
        return jnp.concatenate([x, jnp.zeros((x.shape[0], LANES - t_new), x.dtype)], axis=1)

    def per_head(h, carry):
        head = pl.program_id(1) * hb + h
        score_tiles = []
        for g, (win, dil) in enumerate(DIL_PATTERNS):
            ck = cache_refs[2 * g]
            n_tiles = win // LANES
            per_pass = min(dil, t_new)
            q_t = qn_ref[g, h] * scale
            k_new = widen(kn_ref[g, h])
            slope = _slope_like((1, LANES), g, head)
            res = lane & (dil - 1)
            tiles = [jnp.zeros((t_new, LANES), jnp.float32) for _ in range(n_tiles + 1)]
            for k in range(t_new // per_pass):
                q_mix = None
                for t in range(k * per_pass, (k + 1) * per_pass):
                    q_col = jnp.broadcast_to(q_t[:, t:t + 1], (HEAD_DIM, LANES))
                    row = jnp.sum(k_new * q_col, axis=0, keepdims=True)
                    tiles[n_tiles] = jnp.where(trow == t, row, tiles[n_tiles])
                    term = q_col if per_pass == 1 else jnp.where(res == t % dil, q_col, 0.0)
                    q_mix = term if q_mix is None else q_mix + term
                in_pass = (trow >= k * per_pass) & (trow < (k + 1) * per_pass)
                for j in range(n_tiles):
                    row = jnp.sum(ck[h, :, j * LANES:(j + 1) * LANES] * q_mix, axis=0, keepdims=True)
                    tiles[j] = jnp.where(in_pass, row, tiles[j])
            for j in range(n_tiles + 1):
                if j < n_tiles:
                    dist = win + trow - (j * LANES + lane)
                    valid = ((dist & (dil - 1)) == 0) & (dist <= win)
                else:
                    dist = trow - lane
                    valid = ((dist & (dil - 1)) == 0) & (dist >= 0)
                tiles[j] = jnp.where(valid, tiles[j] - slope * dist.astype(jnp.float32), NEG_INF)
            score_tiles.append(tiles)
        flat = [s for tiles in score_tiles for s in tiles]
        m = functools.reduce(jnp.maximum, [jnp.max(s, axis=1, keepdims=True) for s in flat])
        denom = functools.reduce(lambda a, b: a + b,
                                 [jnp.sum(jnp.exp(s - m), axis=1, keepdims=True) for s in flat])
        probs = [[jnp.exp(s - m) / denom for s in tiles] for tiles in score_tiles]
        pass_sums = {}
        for g, (win, dil) in enumerate(DIL_PATTERNS):
            cv = cache_refs[2 * g + 1]
            per_pass = min(dil, t_new)
            for k in range(t_new // per_pass):
                in_pass = (trow >= k * per_pass) & (trow < (k + 1) * per_pass)
                acc = None
                for j in range(win // LANES):
                    w_row = jnp.sum(jnp.where(in_pass, probs[g][j], 0.0), axis=0, keepdims=True)
                    term = cv[h, :, j * LANES:(j + 1) * LANES] * w_row
                    acc = term if acc is None else acc + term
                pass_sums[g, k] = acc
        out = jnp.zeros((HEAD_DIM, LANES), jnp.float32)
        for t in range(t_new):
            tot = None
            for g, (win, dil) in enumerate(DIL_PATTERNS):
                per_pass = min(dil, t_new)
                mine = pass_sums[g, t // per_pass]
                if per_pass > 1:
                    mine = jnp.where((lane & (dil - 1)) == t % dil, mine, 0.0)
                mine = mine + widen(vn_ref[g, h]) * probs[g][win // LANES][t:t + 1, :]
                tot = mine if tot is None else tot + mine
            out = jnp.where(lane == t, jnp.sum(tot, axis=1, keepdims=True), out)
        o_ref[h] = out[:, 0:t_new]
        for g, (win, dil) in enumerate(DIL_PATTERNS):
            for src, new, dst in ((cache_refs[2 * g], kn_ref, out_refs[2 * g]),
                                  (cache_refs[2 * g + 1], vn_ref, out_refs[2 * g + 1])):
                rolled = pltpu.roll(src[h], win - t_new, 1)
                top = pltpu.roll(widen(new[g, h]), LANES - t_new, 1)
                if win > LANES:
                    dst[h, :, 0:win - LANES] = rolled[:, 0:win - LANES]
                dst[h, :, win - LANES:win] = jnp.where(lane >= LANES - t_new, top,
                                                       rolled[:, win - LANES:win])
        return carry

    lax.fori_loop(0, hb, per_head, 0)


def _sample_attention(q_new, k_new, v_new, caches_k, caches_v):
    n, _, nh, dh, t_new = q_new.shape
    hb = SAMPLE_HEADS_PER_STEP
    new_spec = pl.BlockSpec((None, N_GROUPS, hb, dh, t_new), lambda b, j: (b, 0, j, 0, 0))
    in_specs, args, out_specs, out_shape = [new_spec] * 3, [q_new, k_new, v_new], [], []
    for g, (win, dil) in enumerate(DIL_PATTERNS):
        assert caches_k[g].shape == (n, nh, dh, win) and win % LANES == 0 and dil & (dil - 1) == 0
        spec = pl.BlockSpec((None, hb, dh, win), lambda b, j: (b, j, 0, 0))
        in_specs += [spec, spec]
        args += [caches_k[g], caches_v[g]]
        out_specs += [spec, spec]
        out_shape += [jax.ShapeDtypeStruct(caches_k[g].shape, jnp.float32)] * 2
    out_specs.append(pl.BlockSpec((None, hb, dh, t_new), lambda b, j: (b, j, 0, 0)))
    out_shape.append(jax.ShapeDtypeStruct((n, nh, dh, t_new), jnp.float32))
    res = pl.pallas_call(
        functools.partial(_sample_kernel, t_new=t_new),
        grid=(n, nh // hb),
        in_specs=in_specs,
        out_specs=out_specs,
        out_shape=out_shape,
        compiler_params=_params(),
        name="sample_attention",
    )(*args)
    return res[-1], res[:-1]


def _catnorm_kernel(att_ref, cy_ref, ga_ref, gc_ref, o_ref):
    wa = att_ref.shape[1]
    o_ref[:, 0:wa] = _rms(att_ref[...], ga_ref[...]).astype(o_ref.dtype)
    o_ref[:, wa:] = _rms(cy_ref[...], gc_ref[...]).astype(o_ref.dtype)


def _catnorm(att, cy, g_att, g_conv, tm):
    t, wa = att.shape
    wc = cy.shape[1]
    return pl.pallas_call(
        _catnorm_kernel,
        grid=(t // tm,),
        in_specs=[pl.BlockSpec((tm, wa), lambda i: (i, 0)),
                  pl.BlockSpec((tm, wc), lambda i: (i, 0)),
                  pl.BlockSpec((1, wa), lambda i: (0, 0)),
                  pl.BlockSpec((1, wc), lambda i: (0, 0))],
        out_specs=pl.BlockSpec((tm, wa + wc), lambda i: (i, 0)),
        out_shape=jax.ShapeDtypeStruct((t, wa + wc), jnp.bfloat16),
        compiler_params=_params(),
        name="catnorm",
    )(att, cy, g_att.reshape(1, wa), g_conv.reshape(1, wc))


def _outproj_kernel(a_ref, w_ref, x_ref, g_ref, h_ref, hn_ref):
    h = x_ref[...] + jnp.dot(a_ref[...], w_ref[...], preferred_element_type=jnp.float32)
    h_ref[...] = h
    hn_ref[...] = _rms(h, g_ref[...]).astype(hn_ref.dtype)


def _outproj(cat, w_out, x, g_ffn, tm):
    t, k = cat.shape
    d = w_out.shape[1]
    return pl.pallas_call(
        _outproj_kernel,
        grid=(t // tm,),
        in_specs=[pl.BlockSpec((tm, k), lambda i: (i, 0)),
                  pl.BlockSpec((k, d), lambda i: (0, 0)),
                  pl.BlockSpec((tm, d), lambda i: (i, 0)),
                  pl.BlockSpec((1, d), lambda i: (0, 0))],
        out_specs=[pl.BlockSpec((tm, d), lambda i: (i, 0)),
                   pl.BlockSpec((tm, d), lambda i: (i, 0))],
        out_shape=[jax.ShapeDtypeStruct((t, d), jnp.float32),
                   jax.ShapeDtypeStruct((t, d), jnp.bfloat16)],
        compiler_params=_params(),
        name="outproj",
    )(cat, w_out, x, g_ffn.reshape(1, d))


_N_RANK = PEER_TOPK + 1
_CAND_PER_RANK = tuple(_N_RANK // (i + 1) for i in range(_N_RANK))
_N_CAND = sum(_CAND_PER_RANK)
_CAND_ROWS = 8 << (-(-_N_CAND // 8) - 1).bit_length()


def _merge_desc(xs):
    n = len(xs)
    if n == 1:
        return xs
    half = n // 2
    hi = [jnp.maximum(xs[i], xs[i + half]) for i in range(half)]
    lo = [jnp.minimum(xs[i], xs[i + half]) for i in range(half)]
    return _merge_desc(hi) + _merge_desc(lo)


def _sort_desc(xs):
    n = len(xs)
    if n == 1:
        return xs
    return _merge_desc(_sort_desc(xs[:n // 2]) + _sort_desc(xs[n // 2:])[::-1])


def _top_sorted(x, k):
    rows, tt = x.shape
    keep = 32
    n = rows // 8
    assert rows == 8 * n and n & (n - 1) == 0 and k <= min(keep, rows)
    outs = []
    for lt in range(tt // LANES):
        xs = _sort_desc([x[8 * i:8 * i + 8, lt * LANES:(lt + 1) * LANES] for i in range(n)])
        for shift in (4, 2, 1):
            other = [pltpu.roll(v, shift, 0) for v in xs]
            m = len(xs)
            if 2 * m <= keep:
                xs = _merge_desc(xs + other[::-1])
            else:
                xs = _merge_desc([jnp.maximum(xs[i], other[m - 1 - i]) for i in range(m)])
        outs.append(jnp.concatenate([v[0:1, :] for v in xs[:k]], axis=0))
    return jnp.concatenate(outs, axis=1)


def _route_kernel(q_ref, ka_ref, kb_ref, thr_ref, sb_ref, ea_ref, eb_ref):
    ka = ka_ref[...]
    kb = kb_ref[...]
    tt = q_ref.shape[0]
    sub = N_KEYS // 8

    def per_head(h, carry):
        c0 = pl.multiple_of(h * PEER_QDIM, PEER_QDIM)
        qa = q_ref[:, pl.ds(c0, PEER_HALF)]
        qb = q_ref[:, pl.ds(c0 + PEER_HALF, PEER_HALF)]
        sa = lax.dot_general(ka, qa, _NT, preferred_element_type=jnp.float32)
        sb = lax.dot_general(kb, qb, _NT, preferred_element_type=jnp.float32)
        va = _top_sorted(sa, _N_RANK)
        vb = _top_sorted(sb, _N_RANK)
        pieces = [va[i:i + 1, :] + vb[0:n, :] for i, n in enumerate(_CAND_PER_RANK)]
        pieces.append(jnp.full((_CAND_ROWS - _N_CAND, tt), -jnp.inf, jnp.float32))
        top = _top_sorted(jnp.concatenate(pieces, axis=0), _N_RANK)
        z = jnp.sum(jnp.exp(top[0:PEER_TOPK, :] - top[0:1, :]), axis=0, keepdims=True)
        tau = 0.5 * (top[PEER_TOPK - 1:PEER_TOPK, :] + top[PEER_TOPK:PEER_TOPK + 1, :])
        thr = tau - sa
        ea = jnp.exp(sa - va[0:1, :])
        eb = jnp.exp(sb - vb[0:1, :]) / z
        for lt in range(tt // LANES):
            ls = slice(lt * LANES, (lt + 1) * LANES)
            thr_ref[h, lt] = thr[:, ls].reshape(sub, 8, LANES)
            ea_ref[h, lt] = ea[:, ls].reshape(sub, 8, LANES)
            sb_ref[h, lt] = sb[:, ls]
            eb_ref[h, lt] = eb[:, ls]
        return carry

    lax.fori_loop(0, PEER_HEADS, per_head, 0)


def _route_specs(tt, imap):
    n_lt = tt // LANES
    row = pl.BlockSpec((PEER_HEADS, n_lt, N_KEYS // 8, 8, LANES), lambda *g: (0, imap(*g), 0, 0, 0))
    full = pl.BlockSpec((PEER_HEADS, n_lt, N_KEYS, LANES), lambda *g: (0, imap(*g), 0, 0))
    return [row, full, row, full]


def _route(qp, ka, kb, tt):
    t = qp.shape[0]
    row_shape = jax.ShapeDtypeStruct((PEER_HEADS, t // LANES, N_KEYS // 8, 8, LANES), jnp.float32)
    full_shape = jax.ShapeDtypeStruct((PEER_HEADS, t // LANES, N_KEYS, LANES), jnp.float32)
    return pl.pallas_call(
        _route_kernel,
        grid=(t // tt,),
        in_specs=[pl.BlockSpec((tt, PEER_HEADS * PEER_QDIM), lambda i: (i, 0)),
                  pl.BlockSpec((N_KEYS, PEER_HALF), lambda i: (0, 0)),
                  pl.BlockSpec((N_KEYS, PEER_HALF), lambda i: (0, 0))],
        out_specs=_route_specs(tt, lambda i: i),
        out_shape=[row_shape, full_shape, row_shape, full_shape],
        compiler_params=_params(),
        name="peer_route",
    )(qp, ka, kb)


def _gelu(x):
    return 0.5 * x * (1.0 + lax.erf(x * math.sqrt(0.5)))


def _peer_kernel(hn_ref, u_ref, v_ref, thr_ref, sb_ref, ea_ref, eb_ref, o_ref, h_scr, w_scr):
    e = pl.program_id(1)
    et, tt = h_scr.shape
    assert et == 8 * N_KEYS

    @pl.when(e == 0)
    def _():
        o_ref[...] = jnp.zeros(o_ref.shape, o_ref.dtype)

    h_scr[...] = jnp.dot(u_ref[...], hn_ref[...],
                         preferred_element_type=jnp.float32)

    def per_lane_tile(lt, carry):
        ls = pl.ds(pl.multiple_of(lt * LANES, LANES), LANES)
        for a in range(et // N_KEYS):
            rows = pl.ds(a * N_KEYS, N_KEYS)
            acc = None
            for h in range(PEER_HEADS):
                keep = sb_ref[h, lt] >= thr_ref[h, lt, e, a:a + 1, :]
                gate = jnp.where(keep, eb_ref[h, lt], 0.0) * ea_ref[h, lt, e, a:a + 1, :]
                acc = gate if acc is None else acc + gate
            w_scr[rows, ls] = (acc * _gelu(h_scr[rows, ls])).astype(w_scr.dtype)
        return carry

    lax.fori_loop(0, tt // LANES, per_lane_tile, 0)
    o_ref[...] += lax.dot_general(w_scr[...], v_ref[...], _TN,
                                  preferred_element_type=jnp.float32)


def _peer_dense(hn, u, v, routing, tt, et):
    t, d = hn.shape
    n_exp = u.shape[0]
    return pl.pallas_call(
        _peer_kernel,
        grid=(t // tt, n_exp // et),
        in_specs=[pl.BlockSpec((d, tt), lambda i, e: (0, i)),
                  pl.BlockSpec((et, d), lambda i, e: (e, 0)),
                  pl.BlockSpec((et, d), lambda i, e: (e, 0))] + _route_specs(tt, lambda i, e: i),
        out_specs=pl.BlockSpec((tt, d), lambda i, e: (i, 0)),
        out_shape=jax.ShapeDtypeStruct((t, d), jnp.float32),
        scratch_shapes=[pltpu.VMEM((et, tt), jnp.float32),
                        pltpu.VMEM((et, tt), jnp.bfloat16)],
        compiler_params=_params(),
        name="peer_dense",
    )(hn.T, u, v, *routing)


def _final_kernel(h_ref, p_ref, g_ref, o_ref):
    o_ref[...] = _rms(h_ref[...] + p_ref[...], g_ref[...])


def _final(h, p, g, tm):
    t, d = h.shape
    row = pl.BlockSpec((tm, d), lambda i: (i, 0))
    return pl.pallas_call(
        _final_kernel,
        grid=(t // tm,),
        in_specs=[row, row, pl.BlockSpec((1, d), lambda i: (0, 0))],
        out_specs=row,
        out_shape=jax.ShapeDtypeStruct((t, d), jnp.float32),
        compiler_params=_params(),
        name="final_norm",
    )(h, p, g.reshape(1, d))


def _tile(t, pref):
    while t % pref:
        pref //= 2
    return pref


def _project_in(x2, norm_g, w_in_bf):
    t = x2.shape[0]
    xn = _rmsnorm(x2, norm_g, jnp.bfloat16, _tile(t, 256))
    tm = _tile(t, 1024)
    cw = (w_in_bf.shape[1] - 3 * QKV_WIDTH)
    q = _matmul(xn, w_in_bf, 0, QKV_WIDTH, jnp.float32, tm, 1024)
    k = _matmul(xn, w_in_bf, QKV_WIDTH, QKV_WIDTH, jnp.float32, tm, 1024)
    v = _matmul(xn, w_in_bf, 2 * QKV_WIDTH, QKV_WIDTH, jnp.float32, tm, 1024)
    pc = _matmul(xn, w_in_bf, 3 * QKV_WIDTH, cw, jnp.float32, tm, 1024)
    return q, k, v, pc, xn


def _channel_mix(x2, att, cy, lw):
    t = x2.shape[0]
    cat = _catnorm(att, cy, lw["g_att"], lw["g_conv"], _tile(t, 256))
    h, hn = _outproj(cat, lw["w_out"], x2, lw["g_ffn"], _tile(t, 512))
    qp = _matmul(hn, lw["w_pq"], 0, lw["w_pq"].shape[1], jnp.bfloat16, _tile(t, 1024), 1024)
    routing = _route(qp, lw["ka"], lw["kb"], _tile(t, 512))
    peer = _peer_dense(hn, lw["u"], lw["v"], routing, _tile(t, 512), 8 * N_KEYS)
    return h, peer


def kernel(x_prompt, x_sample, cache_k_g0, cache_v_g0, cache_k_g1, cache_v_g1, cache_k_g2, cache_v_g2, state_conv, norm_mix, w_in, conv_w, norm_att_out, norm_conv_out, w_out, norm_ffn, w_pq, sub_keys_a, sub_keys_b, expert_u, expert_v, norm_final):
    bf = jnp.bfloat16
    depth = w_in.shape[0]
    b, s, d = x_prompt.shape
    bd, td, _ = x_sample.shape
    caches_k = (cache_k_g0, cache_k_g1, cache_k_g2)
    caches_v = (cache_v_g0, cache_v_g1, cache_v_g2)
    conv_c = conv_w.shape[2]

    hp = x_prompt.reshape(b * s, d)
    hs = x_sample.reshape(bd * td, d)
    nk_p = [[] for _ in range(N_GROUPS)]
    nv_p = [[] for _ in range(N_GROUPS)]
    nk_s = [[] for _ in range(N_GROUPS)]
    nv_s = [[] for _ in range(N_GROUPS)]
    nc_p, nc_s = [], []
    for l in range(depth):
        lw = dict(g_att=norm_att_out[l], g_conv=norm_conv_out[l], w_out=w_out[l].astype(bf),
                  g_ffn=norm_ffn[l], w_pq=w_pq[l].astype(bf), ka=sub_keys_a[l].astype(bf),
                  kb=sub_keys_b[l].astype(bf), u=expert_u[l].astype(bf), v=expert_v[l].astype(bf))
        w_in_bf = w_in[l].astype(bf)

        q, k, v, pc, xn = _project_in(hp, norm_mix[l], w_in_bf)
        att = _prompt_attention(q, k, v, b, s)
        cy, ns = _gated_conv(pc, jnp.zeros((b, CONV_K - 1, conv_c), jnp.float32), conv_w[l], b, s, 256)
        for g, (win, _) in enumerate(DIL_PATTERNS):
            rows = min(win, s)
            for dst, base in ((nk_p, QKV_WIDTH), (nv_p, 2 * QKV_WIDTH)):
                proj_t = _matmul_t(w_in_bf, xn, base + g * ATT_WIDTH, ATT_WIDTH, b, s, rows, 1024)
                dst[g].append(jnp.transpose(proj_t.reshape(b, ATT_SLOTS, HEAD_DIM, rows), (0, 3, 1, 2)))
        nc_p.append(ns)
        h_res, peer = _channel_mix(hp, att, cy, lw)
        last = l == depth - 1
        if last:
            y_prompt = _final(h_res, peer, norm_final, _tile(b * s, 256))
        else:
            hp = h_res + peer

        q, k, v, pc, _ = _project_in(hs, norm_mix[l], w_in_bf)
        new_t = lambda a: jnp.transpose(a.reshape(bd, td, N_GROUPS, ATT_SLOTS, HEAD_DIM), (0, 2, 3, 4, 1))
        ck_t = [jnp.transpose(c[l], (0, 2, 3, 1)) for c in caches_k]
        cv_t = [jnp.transpose(c[l], (0, 2, 3, 1)) for c in caches_v]
        att_t, rolled = _sample_attention(new_t(q), new_t(k), new_t(v), ck_t, cv_t)
        att = jnp.transpose(att_t, (0, 3, 1, 2)).reshape(bd * td, ATT_WIDTH)
        for g in range(N_GROUPS):
            nk_s[g].append(jnp.transpose(rolled[2 * g], (0, 3, 1, 2)))
            nv_s[g].append(jnp.transpose(rolled[2 * g + 1], (0, 3, 1, 2)))
        cy, ns = _gated_conv(pc, state_conv[l], conv_w[l], bd, td, conv_c)
        nc_s.append(ns)
        h_res, peer = _channel_mix(hs, att, cy, lw)
        if last:
            y_sample = _final(h_res, peer, norm_final, _tile(bd * td, 256))
        else:
            hs = h_res + peer

    return (y_prompt.reshape(b, s, d), y_sample.reshape(bd, td, d),
            jnp.stack(nk_p[0], 0), jnp.stack(nv_p[0], 0),
            jnp.stack(nk_p[1], 0), jnp.stack(nv_p[1], 0),
            jnp.stack(nk_p[2], 0), jnp.stack(nv_p[2], 0),
            jnp.stack(nc_p, 0),
            jnp.stack(nk_s[0], 0), jnp.stack(nv_s[0], 0),
            jnp.stack(nk_s[1], 0), jnp.stack(nv_s[1], 0),
            jnp.stack(nk_s[2], 0), jnp.stack(nv_s[2], 0),
            jnp.stack(nc_s, 0))
```
